```python
import jax, jax.numpy as jnp
from jax import lax
import numpy as np


D_MODEL = 2048
BATCH = 2
SEQ = 16384
DEPTH = 2

HEAD_DIM = 128
N_HEADS = D_MODEL // HEAD_DIM
N_KV_HEADS = 4
GROUP = N_HEADS // N_KV_HEADS
ROPE_DIM = HEAD_DIM // 4
ROPE_THETA = 500000.0
IDX_HEADS = 16
IDX_DIM = 64
IDX_ROPE_DIM = IDX_DIM // 4
TOPK_MAX = 256
CONV_CH = D_MODEL
CONV_WIDTH = 31
FFN_HIDDEN = -(-(8 * D_MODEL) // (3 * 256)) * 256
Q_BLOCK = 128
EPS = 1e-6

C_Q1 = N_HEADS * HEAD_DIM
C_K1 = C_Q1 + N_KV_HEADS * HEAD_DIM
C_V1 = C_K1 + N_KV_HEADS * HEAD_DIM
C_IQ1 = C_V1 + IDX_HEADS * IDX_DIM
C_IK1 = C_IQ1 + IDX_DIM
C_IW1 = C_IK1 + IDX_HEADS
C_GLU1 = C_IW1 + 2 * CONV_CH
C_GATE1 = C_GLU1 + 2 * D_MODEL
N_IN_COLS = C_GATE1

kernel_name = 'dsa_conformer_gated_hybrid'


def rms_norm(x, g):
    x32 = x.astype(jnp.float32)
    y = x32 * lax.rsqrt(jnp.mean(x32 * x32, axis=-1, keepdims=True) + EPS)
    return (y * g.astype(jnp.float32)).astype(x.dtype)


def layer_norm(x, g, b):
    x32 = x.astype(jnp.float32)
    mu = jnp.mean(x32, axis=-1, keepdims=True)
    xc = x32 - mu
    y = xc * lax.rsqrt(jnp.mean(xc * xc, axis=-1, keepdims=True) + EPS)
    return (y * g.astype(jnp.float32) + b.astype(jnp.float32)).astype(x.dtype)


def partial_rope(x, pos, rot_dim):
    half = rot_dim // 2
    freqs = ROPE_THETA ** (-jnp.arange(half, dtype=jnp.float32) / half)
    ang = pos.astype(jnp.float32)[..., None] * freqs
    ang = ang.reshape(ang.shape[:2] + (1,) * (x.ndim - 3) + (half,))
    cos = jnp.cos(ang).astype(x.dtype)
    sin = jnp.sin(ang).astype(x.dtype)
    x1 = x[..., :half]
    x2 = x[..., half:rot_dim]
    return jnp.concatenate([x1 * cos - x2 * sin, x2 * cos + x1 * sin, x[..., rot_dim:]], axis=-1)


def causal_depthwise_conv(u, w, b):
    out = lax.conv_general_dilated(
        u, w[:, None, :].astype(u.dtype), window_strides=(1,),
        padding=[(CONV_WIDTH - 1, 0)],
        dimension_numbers=('NWC', 'WIO', 'NWC'),
        feature_group_count=u.shape[-1])
    return out + b


def sparse_attention(q, k, v, iq, ik, iw):
    B, S = q.shape[:2]
    topk = min(TOPK_MAX, S // 4)
    nb = S // Q_BLOCK
    key_idx = jnp.arange(S)
    gather = jax.vmap(lambda a, i: a[i])

    def block(args):
        qb, iqb, iwb, t = args
        sc = jax.nn.relu(jnp.einsum('bqhd,bsd->bqhs', iqb, ik))
        score = jnp.einsum('bqh,bqhs->bqs', iwb, sc).astype(jnp.float32)
        score = jnp.where(key_idx[None, None, :] <= t[None, :, None], score, -jnp.inf)
        _, idx = lax.top_k(score, topk)
        kg = gather(k, idx)
        vg = gather(v, idx)
        s = jnp.einsum('bqhgd,bqnhd->bqhgn', qb, kg).astype(jnp.float32) * (HEAD_DIM ** -0.5)
        valid = (idx <= t[None, :, None])[:, :, None, None, :]
        p = jax.nn.softmax(jnp.where(valid, s, -jnp.inf), axis=-1).astype(vg.dtype)
        o = jnp.einsum('bqhgn,bqnhd->bqhgd', p, vg)
        return o.reshape(B, Q_BLOCK, N_HEADS * HEAD_DIM)

    def blocks(a):
        return a.reshape((B, nb, Q_BLOCK) + a.shape[2:]).swapaxes(0, 1)

    out = lax.map(block, (blocks(q), blocks(iq), blocks(iw), key_idx.reshape(nb, Q_BLOCK)))
    return out.swapaxes(0, 1).reshape(B, S, N_HEADS * HEAD_DIM)


def hybrid_mixer(h, pos, w_in, ik_g, ik_b, dw_w, dw_b, cln_g, cln_b, w_co, b_co, w_out):
    B, S, _ = h.shape

    def proj(lo, hi):
        return jnp.einsum('bsd,de->bse', h, w_in[:, lo:hi])

    q = partial_rope(proj(0, C_Q1).reshape(B, S, N_HEADS, HEAD_DIM), pos, ROPE_DIM)
    q = q.reshape(B, S, N_KV_HEADS, GROUP, HEAD_DIM)
    k = partial_rope(proj(C_Q1, C_K1).reshape(B, S, N_KV_HEADS, HEAD_DIM), pos, ROPE_DIM)
    v = proj(C_K1, C_V1).reshape(B, S, N_KV_HEADS, HEAD_DIM)
    iq = partial_rope(proj(C_V1, C_IQ1).reshape(B, S, IDX_HEADS, IDX_DIM), pos, IDX_ROPE_DIM)
    ik = partial_rope(layer_norm(proj(C_IQ1, C_IK1), ik_g, ik_b), pos, IDX_ROPE_DIM)
    iw = proj(C_IK1, C_IW1) * ((IDX_HEADS ** -0.5) * (IDX_DIM ** -0.5))
    attn_out = sparse_attention(q, k, v, iq, ik, iw)

    u_a, u_g = jnp.split(proj(C_IW1, C_GLU1), 2, axis=-1)
    u = causal_depthwise_conv(u_a * jax.nn.sigmoid(u_g), dw_w, dw_b)
    u = jax.nn.silu(layer_norm(u, cln_g, cln_b))
    conv_out = jnp.einsum('bsc,cd->bsd', u, w_co) + b_co

    g_attn, g_conv = jnp.split(jax.nn.sigmoid(proj(C_GLU1, C_GATE1)), 2, axis=-1)
    y = g_attn * attn_out + g_conv * conv_out
    return jnp.einsum('bsd,de->bse', y, w_out)


def swiglu(h, w_gate, w_up, w_down):
    a = jnp.einsum('bsd,df->bsf', h, w_gate)
    b = jnp.einsum('bsd,df->bsf', h, w_up)
    return jnp.einsum('bsf,fd->bsd', jax.nn.silu(a) * b, w_down)


def setup_inputs(seed: int = 0) -> dict:
    key = jax.random.key(seed)
    ks = jax.random.split(key, 20)

    def nrm(k, shape, scale):
        return jax.random.normal(k, shape, jnp.float32) * scale

    return {
        'x': nrm(ks[0], (BATCH, SEQ, D_MODEL), 1.0),
        'positions': jnp.broadcast_to(jnp.arange(SEQ, dtype=jnp.int32), (BATCH, SEQ)),
        'norm_mix_g': 1.0 + nrm(ks[1], (DEPTH, D_MODEL), 0.02),
        'w_in': nrm(ks[2], (DEPTH, D_MODEL, N_IN_COLS), D_MODEL ** -0.5),
        'idx_k_ln_g': 1.0 + nrm(ks[3], (DEPTH, IDX_DIM), 0.02),
        'idx_k_ln_b': nrm(ks[4], (DEPTH, IDX_DIM), 0.02),
        'conv_dw_w': nrm(ks[5], (DEPTH, CONV_WIDTH, CONV_CH), CONV_WIDTH ** -0.5),
        'conv_dw_b': nrm(ks[6], (DEPTH, CONV_CH), 0.02),
        'conv_ln_g': 1.0 + nrm(ks[7], (DEPTH, CONV_CH), 0.02),
        'conv_ln_b': nrm(ks[8], (DEPTH, CONV_CH), 0.02),
        'w_conv_out': nrm(ks[9], (DEPTH, CONV_CH, D_MODEL), CONV_CH ** -0.5),
        'b_conv_out': nrm(ks[10], (DEPTH, D_MODEL), 0.02),
        'w_out': nrm(ks[11], (DEPTH, D_MODEL, D_MODEL), D_MODEL ** -0.5),
        'norm_ffn_g': 1.0 + nrm(ks[12], (DEPTH, D_MODEL), 0.02),
        'w_ffn_gate': nrm(ks[13], (DEPTH, D_MODEL, FFN_HIDDEN), D_MODEL ** -0.5),
        'w_ffn_up': nrm(ks[14], (DEPTH, D_MODEL, FFN_HIDDEN), D_MODEL ** -0.5),
        'w_ffn_down': nrm(ks[15], (DEPTH, FFN_HIDDEN, D_MODEL), FFN_HIDDEN ** -0.5),
        'final_norm_g': 1.0 + nrm(ks[16], (D_MODEL,), 0.02),
    }


def reference(x, positions, norm_mix_g, w_in, idx_k_ln_g, idx_k_ln_b, conv_dw_w, conv_dw_b,
              conv_ln_g, conv_ln_b, w_conv_out, b_conv_out, w_out, norm_ffn_g,
              w_ffn_gate, w_ffn_up, w_ffn_down, final_norm_g):
    for l in range(DEPTH):
        h = rms_norm(x, norm_mix_g[l])
        x = x + hybrid_mixer(h, positions, w_in[l], idx_k_ln_g[l], idx_k_ln_b[l],
                             conv_dw_w[l], conv_dw_b[l], conv_ln_g[l], conv_ln_b[l],
                             w_conv_out[l], b_conv_out[l], w_out[l])
        h = rms_norm(x, norm_ffn_g[l])
        x = x + swiglu(h, w_ffn_gate[l], w_ffn_up[l], w_ffn_down[l])
    return rms_norm(x, final_norm_g)
```

```python
import functools
import math

import numpy as np
import jax
import jax.numpy as jnp
from jax import lax
from jax.experimental import pallas as pl
from jax.experimental.pallas import tpu as pltpu

HEAD_DIM = 128
N_KV_HEADS = 4
ROPE_DIM = HEAD_DIM // 4
ROPE_THETA = 500000.0
IDX_HEADS = 16
IDX_DIM = 64
IDX_ROPE_DIM = IDX_DIM // 4
TOPK_MAX = 256
CONV_WIDTH = 31
EPS = 1e-6

LANES = 128
SUBLANES = 8
VMEM_LIMIT = 56 * 1024 * 1024

TQ = 256
TK = 512
TKI = 128
CNT_ROWS = 128
CNT_ACC = 32
HALO = 32
MASK_NEG = -1e30
F32_LOWEST = float(np.finfo(np.float32).min)
MAX_BISECT = 200

_NT = (((1,), (1,)), ((), ()))


def _cparams(sem):
    return pltpu.CompilerParams(dimension_semantics=sem, vmem_limit_bytes=VMEM_LIMIT)


def _rms_to_bf16(x, g):
    ms = jnp.mean(x * x, axis=-1, keepdims=True)
    return (x * lax.rsqrt(ms + EPS) * g).astype(jnp.bfloat16)


def _norm_proj_kernel(x_ref, g_ref, w_ref, o_ref, h_scr):
    @pl.when(pl.program_id(1) == 0)
    def _():
        h_scr[...] = _rms_to_bf16(x_ref[...], g_ref[...])

    o_ref[...] = jnp.dot(h_scr[...], w_ref[...],
                         preferred_element_type=jnp.float32).astype(o_ref.dtype)


def _norm_proj(x, g, w, out_dtype, tm, tn):
    m, d = x.shape
    n = w.shape[1]
    return pl.pallas_call(
        _norm_proj_kernel,
        grid=(m // tm, n // tn),
        in_specs=[pl.BlockSpec((tm, d), lambda i, j: (i, 0)),
                  pl.BlockSpec((1, d), lambda i, j: (0, 0)),
                  pl.BlockSpec((d, tn), lambda i, j: (0, j))],
        out_specs=pl.BlockSpec((tm, tn), lambda i, j: (i, j)),
        out_shape=jax.ShapeDtypeStruct((m, n), out_dtype),
        scratch_shapes=[pltpu.VMEM((tm, d), jnp.bfloat16)],
        compiler_params=_cparams(("parallel", "arbitrary")),
        name="norm_proj",
    )(x, g, w)


def _ffn_up_kernel(x_ref, g_ref, wg_ref, wu_ref, o_ref, h_scr):
    @pl.when(pl.program_id(1) == 0)
    def _():
        h_scr[...] = _rms_to_bf16(x_ref[...], g_ref[...])

    h = h_scr[...]
    a = jnp.dot(h, wg_ref[...], preferred_element_type=jnp.float32)
    b = jnp.dot(h, wu_ref[...], preferred_element_type=jnp.float32)
    o_ref[...] = (a * jax.nn.sigmoid(a) * b).astype(o_ref.dtype)


def _ffn_up(x, g, wg, wu, tm, tn):
    m, d = x.shape
    n = wg.shape[1]
    return pl.pallas_call(
        _ffn_up_kernel,
        grid=(m // tm, n // tn),
        in_specs=[pl.BlockSpec((tm, d), lambda i, j: (i, 0)),
                  pl.BlockSpec((1, d), lambda i, j: (0, 0)),
                  pl.BlockSpec((d, tn), lambda i, j: (0, j)),
                  pl.BlockSpec((d, tn), lambda i, j: (0, j))],
        out_specs=pl.BlockSpec((tm, tn), lambda i, j: (i, j)),
        out_shape=jax.ShapeDtypeStruct((m, n), jnp.bfloat16),
        scratch_shapes=[pltpu.VMEM((tm, d), jnp.bfloat16)],
        compiler_params=_cparams(("parallel", "arbitrary")),
        name="ffn_up",
    )(x, g, wg, wu)


def _mm_res_kernel(a_ref, w_ref, r_ref, o_ref):
    o_ref[...] = r_ref[...] + jnp.dot(a_ref[...], w_ref[...],
                                      preferred_element_type=jnp.float32)


def _mm_res(a, w, res, tm, tn):
    m, k = a.shape
    n = w.shape[1]
    return pl.pallas_call(
        _mm_res_kernel,
        grid=(m // tm, n // tn),
        in_specs=[pl.BlockSpec((tm, k), lambda i, j: (i, 0)),
                  pl.BlockSpec((k, tn), lambda i, j: (0, j)),
                  pl.BlockSpec((tm, tn), lambda i, j: (i, j))],
        out_specs=pl.BlockSpec((tm, tn), lambda i, j: (i, j)),
        out_shape=jax.ShapeDtypeStruct((m, n), jnp.float32),
        compiler_params=_cparams(("parallel", "arbitrary")),
        name="mm_res",
    )(a, w, res)


def _mix_kernel(u_ref, w_ref, b_ref, ga_ref, gc_ref, at_ref, y_ref):
    conv = jnp.dot(u_ref[...], w_ref[...], preferred_element_type=jnp.float32) + b_ref[...]
    ga = jax.nn.sigmoid(ga_ref[...].astype(jnp.float32))
    gc = jax.nn.sigmoid(gc_ref[...].astype(jnp.float32))
    y_ref[...] = (ga * at_ref[...].astype(jnp.float32) + gc * conv).astype(y_ref.dtype)


def _mix(u, w, b, proj, ga_col, gc_col, attn, tm, tn):
    m, k = u.shape
    n = w.shape[1]
    ga_blk = ga_col // tn
    gc_blk = gc_col // tn
    return pl.pallas_call(
        _mix_kernel,
        grid=(m // tm, n // tn),
        in_specs=[pl.BlockSpec((tm, k), lambda i, j: (i, 0)),
                  pl.BlockSpec((k, tn), lambda i, j: (0, j)),
                  pl.BlockSpec((1, tn), lambda i, j: (0, j)),
                  pl.BlockSpec((tm, tn), lambda i, j: (i, ga_blk + j)),
                  pl.BlockSpec((tm, tn), lambda i, j: (i, gc_blk + j)),
                  pl.BlockSpec((tm, tn), lambda i, j: (i, j))],
        out_specs=pl.BlockSpec((tm, tn), lambda i, j: (i, j)),
        out_shape=jax.ShapeDtypeStruct((m, n), jnp.bfloat16),
        compiler_params=_cparams(("parallel", "arbitrary")),
        name="gated_mix",
    )(u, w, b, proj, proj, attn)


def _final_norm_kernel(x_ref, g_ref, o_ref):
    x = x_ref[...]
    ms = jnp.mean(x * x, axis=-1, keepdims=True)
    o_ref[...] = x * lax.rsqrt(ms + EPS) * g_ref[...]


def _final_norm(x, g, tm):
    m, d = x.shape
    return pl.pallas_call(
        _final_norm_kernel,
        grid=(m // tm,),
        in_specs=[pl.BlockSpec((tm, d), lambda i: (i, 0)),
                  pl.BlockSpec((1, d), lambda i: (0, 0))],
        out_specs=pl.BlockSpec((tm, d), lambda i: (i, 0)),
        out_shape=jax.ShapeDtypeStruct((m, d), jnp.float32),
        compiler_params=_cparams(("parallel",)),
        name="final_norm",
    )(x, g)


def _rope(x, c, s1, s2, half):
    return (x * c + pltpu.roll(x, LANES - half, axis=1) * s1
            + pltpu.roll(x, half, axis=1) * s2)


def _prep_kernel(q_ref, k_ref, v_ref, f_ref, th_ref, ti_ref, lng_ref, lnb_ref,
                 qo_ref, ko_ref, vto_ref, iqo_ref, iklo_ref, ikhi_ref, iwo_ref,
                 *, n_heads, q_scale, iw_scale):
    ch, s1h, s2h = th_ref[0], th_ref[1], th_ref[2]
    ci, s1i, s2i = ti_ref[0], ti_ref[1], ti_ref[2]
    hh = ROPE_DIM // 2
    hi = IDX_ROPE_DIM // 2
    for h in range(n_heads):
        xh = q_ref[:, h * HEAD_DIM:(h + 1) * HEAD_DIM].astype(jnp.float32)
        qo_ref[h] = (_rope(xh, ch, s1h, s2h, hh) * q_scale).astype(qo_ref.dtype)
    for g in range(N_KV_HEADS):
        xk = k_ref[:, g * HEAD_DIM:(g + 1) * HEAD_DIM].astype(jnp.float32)
        ko_ref[g] = _rope(xk, ch, s1h, s2h, hh).astype(ko_ref.dtype)
        xv = v_ref[:, g * HEAD_DIM:(g + 1) * HEAD_DIM].astype(jnp.float32)
        vto_ref[g] = xv.T.astype(vto_ref.dtype)
    n_pairs = IDX_HEADS * IDX_DIM // LANES
    for p in range(n_pairs):
        xi = f_ref[:, p * LANES:(p + 1) * LANES]
        iqo_ref[:, p * LANES:(p + 1) * LANES] = _rope(xi, ci, s1i, s2i, hi).astype(iqo_ref.dtype)
    xt = f_ref[:, n_pairs * LANES:(n_pairs + 1) * LANES]
    lane = lax.broadcasted_iota(jnp.int32, xt.shape, 1)
    is_ik = lane < IDX_DIM
    mu = jnp.sum(jnp.where(is_ik, xt, 0.0), axis=-1, keepdims=True) * (1.0 / IDX_DIM)
    xc = jnp.where(is_ik, xt - mu, 0.0)
    var = jnp.sum(xc * xc, axis=-1, keepdims=True) * (1.0 / IDX_DIM)
    y = xc * lax.rsqrt(var + EPS) * lng_ref[...] + lnb_ref[...]
    r = _rope(y, ci, s1i, s2i, hi)
    iklo_ref[...] = r.astype(iklo_ref.dtype)
    ikhi_ref[...] = pltpu.roll(r, IDX_DIM, axis=1).astype(ikhi_ref.dtype)
    iwo_ref[...] = xt * iw_scale


def _prep(proj, fproj, tab_h, tab_i, lng, lnb, n_heads, k_col, v_col, tm):
    b, s, _ = proj.shape
    d = n_heads * HEAD_DIM
    kvw = N_KV_HEADS * HEAD_DIM
    fw = fproj.shape[-1]
    iqw = IDX_HEADS * IDX_DIM
    kern = functools.partial(
        _prep_kernel, n_heads=n_heads,
        q_scale=(HEAD_DIM ** -0.5) * math.log2(math.e),
        iw_scale=(IDX_HEADS ** -0.5) * (IDX_DIM ** -0.5))
    bf = jnp.bfloat16
    return pl.pallas_call(
        kern,
        grid=(b, s // tm),
        in_specs=[pl.BlockSpec((None, tm, d), lambda bi, i: (bi, i, 0)),
                  pl.BlockSpec((None, tm, kvw), lambda bi, i: (bi, i, k_col // kvw)),
                  pl.BlockSpec((None, tm, kvw), lambda bi, i: (bi, i, v_col // kvw)),
                  pl.BlockSpec((None, tm, fw), lambda bi, i: (bi, i, 0)),
                  pl.BlockSpec((None, 3, tm, LANES), lambda bi, i: (bi, 0, i, 0)),
                  pl.BlockSpec((None, 3, tm, LANES), lambda bi, i: (bi, 0, i, 0)),
                  pl.BlockSpec((1, LANES), lambda bi, i: (0, 0)),
                  pl.BlockSpec((1, LANES), lambda bi, i: (0, 0))],
        out_specs=[pl.BlockSpec((None, n_heads, tm, HEAD_DIM), lambda bi, i: (bi, 0, i, 0)),
                   pl.BlockSpec((None, N_KV_HEADS, tm, HEAD_DIM), lambda bi, i: (bi, 0, i, 0)),
                   pl.BlockSpec((None, N_KV_HEADS, HEAD_DIM, tm), lambda bi, i: (bi, 0, 0, i)),
                   pl.BlockSpec((None, tm, iqw), lambda bi, i: (bi, i, 0)),
                   pl.BlockSpec((None, tm, LANES), lambda bi, i: (bi, i, 0)),
                   pl.BlockSpec((None, tm, LANES), lambda bi, i: (bi, i, 0)),
                   pl.BlockSpec((None, tm, LANES), lambda bi, i: (bi, i, 0))],
        out_shape=[jax.ShapeDtypeStruct((b, n_heads, s, HEAD_DIM), bf),
                   jax.ShapeDtypeStruct((b, N_KV_HEADS, s, HEAD_DIM), bf),
                   jax.ShapeDtypeStruct((b, N_KV_HEADS, HEAD_DIM, s), bf),
                   jax.ShapeDtypeStruct((b, s, iqw), bf),
                   jax.ShapeDtypeStruct((b, s, LANES), bf),
                   jax.ShapeDtypeStruct((b, s, LANES), bf),
                   jax.ShapeDtypeStruct((b, s, LANES), jnp.float32)],
        compiler_params=_cparams(("parallel", "parallel")),
        name="rope_prep",
    )(proj, proj, proj, fproj, tab_h, tab_i, lng, lnb)


def _dsa_kernel(bi_ref, qi_ref, kt_ref,
                q_ref, iq_ref, iwt_ref, iklo_ref, ikhi_ref, k_ref, vt_ref,
                o_ref,
                sc_ref, bias_ref, thr_ref, m_ref, l_ref, acc_ref,
                *, n_heads, topk):
    t = pl.program_id(0)
    qi = qi_ref[t]
    kt = kt_ref[t]
    n_keys = (qi + 1) * TQ
    n_kt = (n_keys + TK - 1) // TK
    group = n_heads // N_KV_HEADS
    f32 = jnp.float32

    def score_chunk(r0):
        lo = iklo_ref[pl.ds(r0, TKI), :]
        hi = ikhi_ref[pl.ds(r0, TKI), :]
        acc = jnp.zeros((TKI, TQ), f32)
        for p in range(IDX_HEADS // 2):
            w = iq_ref[:, p * LANES:(p + 1) * LANES]
            ze = lax.dot_general(lo, w, _NT, preferred_element_type=f32)
            zo = lax.dot_general(hi, w, _NT, preferred_element_type=f32)
            acc = acc + iwt_ref[2 * p:2 * p + 1, :] * jnp.maximum(ze, 0.0)
            acc = acc + iwt_ref[2 * p + 1:2 * p + 2, :] * jnp.maximum(zo, 0.0)
        return acc

    def fold8(x, op):
        return op(x.reshape(x.shape[0] // SUBLANES, SUBLANES, x.shape[1]), axis=0)

    @pl.when(kt == 0)
    def _index_and_select():
        def full_chunk(c, carry):
            mx, mn = carry
            r0 = pl.multiple_of(c * TKI, TKI)
            sc = score_chunk(r0)
            sc_ref[pl.ds(r0, TKI), :] = sc
            return (jnp.maximum(mx, fold8(sc, jnp.max)), jnp.minimum(mn, fold8(sc, jnp.min)))

        mx0 = jnp.full((SUBLANES, TQ), -jnp.inf, f32)
        mn0 = jnp.full((SUBLANES, TQ), jnp.inf, f32)
        mx, mn = lax.fori_loop(0, qi * (TQ // TKI), full_chunk, (mx0, mn0))
        for dchunk in range(TQ // TKI):
            r0 = pl.multiple_of(qi * TQ + dchunk * TKI, TKI)
            sc = score_chunk(r0)
            row = lax.broadcasted_iota(jnp.int32, (TKI, TQ), 0) + dchunk * TKI
            col = lax.broadcasted_iota(jnp.int32, (TKI, TQ), 1)
            valid = row <= col
            sc_ref[pl.ds(r0, TKI), :] = jnp.where(valid, sc, -jnp.inf)
            mx = jnp.maximum(mx, fold8(jnp.where(valid, sc, -jnp.inf), jnp.max))
            mn = jnp.minimum(mn, fold8(jnp.where(valid, sc, jnp.inf), jnp.min))

        @pl.when(n_kt * TK > n_keys)
        def _():
            sc_ref[pl.ds(pl.multiple_of(n_keys, TQ), TQ), :] = jnp.full((TQ, TQ), -jnp.inf, f32)

        hi0 = jnp.max(mx, axis=0, keepdims=True)
        lo0 = jnp.min(mn, axis=0, keepdims=True)
        lane = lax.broadcasted_iota(jnp.int32, (1, TQ), 1)
        take_all = (qi * TQ + lane + 1) <= topk
        kf = f32(topk)
        n_cnt = n_keys // CNT_ROWS

        def count_ge(mid):
            def body(c, cnt):
                r0 = pl.multiple_of(c * CNT_ROWS, CNT_ROWS)
                ind = jnp.where(sc_ref[pl.ds(r0, CNT_ROWS), :] >= mid, f32(1), f32(0))
                return cnt + jnp.sum(ind.reshape(CNT_ROWS // CNT_ACC, CNT_ACC, TQ), axis=0)
            cnt = lax.fori_loop(0, n_cnt, body, jnp.zeros((CNT_ACC, TQ), f32))
            return jnp.sum(cnt, axis=0, keepdims=True)

        def cond(st):
            return jnp.logical_and(st[0] < MAX_BISECT, st[1] > 0)

        def body(st):
            it, _, lo, hi, thr, done = st
            mid = lo * 0.5 + hi * 0.5
            stuck = jnp.logical_or(mid <= lo, mid >= hi)
            cnt = count_ge(mid)
            exact = cnt == kf
            ge = cnt >= kf
            act = done < 0.5
            thr = jnp.where(jnp.logical_and(act, exact), mid,
                            jnp.where(jnp.logical_and(act, stuck), lo, thr))
            newly = jnp.logical_and(act, jnp.logical_or(exact, stuck))
            move = jnp.logical_and(act, jnp.logical_not(newly))
            lo = jnp.where(jnp.logical_and(move, ge), mid, lo)
            hi = jnp.where(jnp.logical_and(move, jnp.logical_not(ge)), mid, hi)
            done = jnp.where(newly, f32(1), done)
            n_act = jnp.sum(jnp.where(done < 0.5, jnp.int32(1), jnp.int32(0)))
            return it + 1, n_act, lo, hi, thr, done

        done0 = jnp.where(take_all, f32(1), f32(0))
        thr0 = jnp.full((1, TQ), F32_LOWEST, f32)
        n_act0 = jnp.sum(jnp.where(take_all, jnp.int32(0), jnp.int32(1)))
        st = lax.while_loop(cond, body, (jnp.int32(0), n_act0, lo0, hi0, thr0, done0))
        thr_ref[...] = st[4]

        m_ref[...] = jnp.full(m_ref.shape, MASK_NEG, f32)
        l_ref[...] = jnp.zeros(l_ref.shape, f32)
        acc_ref[...] = jnp.zeros(acc_ref.shape, f32)

    k0 = pl.multiple_of(kt * TK, TK)
    bias_ref[...] = jnp.where(sc_ref[pl.ds(k0, TK), :] >= thr_ref[...], f32(0), f32(MASK_NEG))

    def head_body(h, carry):
        g = h // group
        s = lax.dot_general(k_ref[g], q_ref[h], _NT, preferred_element_type=f32)
        s = s + bias_ref[...]
        m_prev = m_ref[h]
        m_new = jnp.maximum(m_prev, jnp.max(s, axis=0, keepdims=True))
        alpha = jnp.exp2(m_prev - m_new)
        p = jnp.exp2(s - m_new)
        l_ref[h] = alpha * l_ref[h] + jnp.sum(p, axis=0, keepdims=True)
        pv = jnp.dot(vt_ref[g], p.astype(jnp.bfloat16), preferred_element_type=f32)
        acc_ref[h] = alpha * acc_ref[h] + pv
        m_ref[h] = m_new
        return carry

    lax.fori_loop(0, n_heads, head_body, 0)

    @pl.when(kt == n_kt - 1)
    def _finalize():
        for h in range(n_heads):
            o = acc_ref[h] * (1.0 / l_ref[h])
            o_ref[:, h * HEAD_DIM:(h + 1) * HEAD_DIM] = o.T.astype(o_ref.dtype)


def _dsa_steps(b, s):
    bi, qi, kt = [], [], []
    for bb in range(b):
        for q in range(s // TQ):
            for k in range(-(-((q + 1) * TQ) // TK)):
                bi.append(bb)
                qi.append(q)
                kt.append(k)
    return (np.asarray(bi, np.int32), np.asarray(qi, np.int32), np.asarray(kt, np.int32))


def _dsa(q, iq, iwt, iklo, ikhi, k, vt):
    b, n_heads, s, _ = q.shape
    topk = min(TOPK_MAX, s // 4)
    bi, qi, kt = _dsa_steps(b, s)
    iqw = iq.shape[-1]
    kern = functools.partial(_dsa_kernel, n_heads=n_heads, topk=topk)
    grid_spec = pltpu.PrefetchScalarGridSpec(
        num_scalar_prefetch=3,
        grid=(len(bi),),
        in_specs=[
            pl.BlockSpec((None, n_heads, TQ, HEAD_DIM), lambda t, b_, q_, k_: (b_[t], 0, q_[t], 0)),
            pl.BlockSpec((None, TQ, iqw), lambda t, b_, q_, k_: (b_[t], q_[t], 0)),
            pl.BlockSpec((None, IDX_HEADS, TQ), lambda t, b_, q_, k_: (b_[t], 0, q_[t])),
            pl.BlockSpec((None, s, LANES), lambda t, b_, q_, k_: (b_[t], 0, 0)),
            pl.BlockSpec((None, s, LANES), lambda t, b_, q_, k_: (b_[t], 0, 0)),
            pl.BlockSpec((None, N_KV_HEADS, TK, HEAD_DIM), lambda t, b_, q_, k_: (b_[t], 0, k_[t], 0)),
            pl.BlockSpec((None, N_KV_HEADS, HEAD_DIM, TK), lambda t, b_, q_, k_: (b_[t], 0, 0, k_[t])),
        ],
        out_specs=pl.BlockSpec((None, TQ, n_heads * HEAD_DIM), lambda t, b_, q_, k_: (b_[t], q_[t], 0)),
        scratch_shapes=[
            pltpu.VMEM((s, TQ), jnp.float32),
            pltpu.VMEM((TK, TQ), jnp.float32),
            pltpu.VMEM((1, TQ), jnp.float32),
            pltpu.VMEM((n_heads, 1, TQ), jnp.float32),
            pltpu.VMEM((n_heads, 1, TQ), jnp.float32),
            pltpu.VMEM((n_heads, HEAD_DIM, TQ), jnp.float32),
        ],
    )
    return pl.pallas_call(
        kern,
        grid_spec=grid_spec,
        out_shape=jax.ShapeDtypeStruct((b, s, n_heads * HEAD_DIM), jnp.bfloat16),
        compiler_params=_cparams(("arbitrary",)),
        name="dsa_attention",
    )(jnp.asarray(bi), jnp.asarray(qi), jnp.asarray(kt), q, iq, iwt, iklo, ikhi, k, vt)


CONV_ROWS = 32
CONV_COLS = 512


def _conv_kernel(ua_ref, ug_ref, ha_ref, hg_ref, w_ref, b_ref, g_ref, beta_ref, o_ref,
                 glu_scr, y_scr, *, ts):
    i = pl.program_id(1)
    f32 = jnp.float32
    ua = ua_ref[...].astype(f32)
    ug = ug_ref[...].astype(f32)
    glu_scr[HALO:HALO + ts, :] = ua * jax.nn.sigmoid(ug)
    ha = ha_ref[...].astype(f32)
    hg = hg_ref[...].astype(f32)
    halo = ha * jax.nn.sigmoid(hg)
    glu_scr[0:HALO, :] = jnp.where(i > 0, halo, 0.0)

    c = y_scr.shape[1]
    off = HALO - (CONV_WIDTH - 1)
    for cs in range(0, c, CONV_COLS):
        wcs = w_ref[:, cs:cs + CONV_COLS]
        for r0 in range(0, ts, CONV_ROWS):
            acc = jnp.zeros((CONV_ROWS, CONV_COLS), f32) + b_ref[:, cs:cs + CONV_COLS]
            for j in range(CONV_WIDTH):
                acc = acc + wcs[j:j + 1, :] * glu_scr[r0 + off + j:r0 + off + j + CONV_ROWS, cs:cs + CONV_COLS]
            y_scr[r0:r0 + CONV_ROWS, cs:cs + CONV_COLS] = acc

    y = y_scr[...]
    mu = jnp.mean(y, axis=-1, keepdims=True)
    yc = y - mu
    var = jnp.mean(yc * yc, axis=-1, keepdims=True)
    z = yc * lax.rsqrt(var + EPS) * g_ref[...] + beta_ref[...]
    o_ref[...] = (z * jax.nn.sigmoid(z)).astype(o_ref.dtype)


def _conv_module(proj, ua_col, ug_col, w, bias, g, beta, ts):
    b, s, _ = proj.shape
    c = w.shape[1]
    hb = ts // HALO
    kern = functools.partial(_conv_kernel, ts=ts)
    return pl.pallas_call(
        kern,
        grid=(b, s // ts),
        in_specs=[pl.BlockSpec((None, ts, c), lambda bi, i: (bi, i, ua_col // c)),
                  pl.BlockSpec((None, ts, c), lambda bi, i: (bi, i, ug_col // c)),
                  pl.BlockSpec((None, HALO, c), lambda bi, i: (bi, jnp.maximum(i * hb - 1, 0), ua_col // c)),
                  pl.BlockSpec((None, HALO, c), lambda bi, i: (bi, jnp.maximum(i * hb - 1, 0), ug_col // c)),
                  pl.BlockSpec((HALO, c), lambda bi, i: (0, 0)),
                  pl.BlockSpec((1, c), lambda bi, i: (0, 0)),
                  pl.BlockSpec((1, c), lambda bi, i: (0, 0)),
                  pl.BlockSpec((1, c), lambda bi, i: (0, 0))],
        out_specs=pl.BlockSpec((None, ts, c), lambda bi, i: (bi, i, 0)),
        out_shape=jax.ShapeDtypeStruct((b, s, c), jnp.bfloat16),
        scratch_shapes=[pltpu.VMEM((HALO + ts, c), jnp.float32),
                        pltpu.VMEM((ts, c), jnp.float32)],
        compiler_params=_cparams(("parallel", "arbitrary")),
        name="conv_module",
    )(proj, proj, proj, proj, w, bias, g, beta)


def _rope_tables(pos, rot_dim, period):
    half = rot_dim // 2
    freqs = ROPE_THETA ** (-jnp.arange(half, dtype=jnp.float32) / half)
    ang = pos.astype(jnp.float32)[..., None] * freqs
    cos, sin = jnp.cos(ang), jnp.sin(ang)
    pad = period - rot_dim
    z = jnp.zeros(ang.shape[:2] + (pad,), jnp.float32)
    zh = jnp.zeros_like(sin)
    c = jnp.concatenate([cos, cos, z + 1.0], axis=-1)
    s1 = jnp.concatenate([-sin, zh, z], axis=-1)
    s2 = jnp.concatenate([zh, sin, z], axis=-1)
    tab = jnp.stack([c, s1, s2], axis=1)
    return jnp.tile(tab, (1, 1, 1, LANES // period))


def _pad_lanes(v, width):
    return jnp.pad(v, (0, width - v.shape[0]))[None, :].astype(jnp.float32)


def kernel(x, positions, norm_mix_g, w_in, idx_k_ln_g, idx_k_ln_b, conv_dw_w, conv_dw_b, conv_ln_g, conv_ln_b, w_conv_out, b_conv_out, w_out, norm_ffn_g, w_ffn_gate, w_ffn_up, w_ffn_down, final_norm_g):
    b, s, d = x.shape
    depth = w_in.shape[0]
    n_heads = d // HEAD_DIM
    kvw = N_KV_HEADS * HEAD_DIM
    iqw = IDX_HEADS * IDX_DIM
    conv_ch = conv_dw_w.shape[-1]
    m = b * s
    bf = jnp.bfloat16

    c_q1 = d
    c_k1 = c_q1 + kvw
    c_v1 = c_k1 + kvw
    c_iq1 = c_v1 + iqw
    c_ik1 = c_iq1 + IDX_DIM
    c_iw1 = c_ik1 + IDX_HEADS
    c_glu1 = c_iw1 + 2 * conv_ch
    c_gate1 = c_glu1 + 2 * d
    ua_col, ug_col = d, d + conv_ch
    ga_col = d + 2 * conv_ch
    gc_col = ga_col + d
    k_col = gc_col + d
    v_col = k_col + kvw

    tab_h = _rope_tables(positions, ROPE_DIM, HEAD_DIM)
    tab_i = _rope_tables(positions, IDX_ROPE_DIM, IDX_DIM)

    tm = min(1024, m)
    xf = x.reshape(m, d)
    for l in range(depth):
        wl = w_in[l]
        w_slab = jnp.concatenate(
            [wl[:, :c_q1], wl[:, c_iw1:c_glu1], wl[:, c_glu1:c_gate1], wl[:, c_q1:c_v1]],
            axis=1).astype(bf)
        w_idx = jnp.pad(wl[:, c_v1:c_iw1], ((0, 0), (0, LANES - IDX_DIM - IDX_HEADS))).astype(bf)
        g_mix = norm_mix_g[l][None, :]

        proj = _norm_proj(xf, g_mix, w_slab, bf, tm, 512).reshape(b, s, -1)
        fproj = _norm_proj(xf, g_mix, w_idx, jnp.float32, tm, w_idx.shape[1]).reshape(b, s, -1)

        q, k, vt, iq, iklo, ikhi, iwp = _prep(
            proj, fproj, tab_h, tab_i,
            _pad_lanes(idx_k_ln_g[l], LANES), _pad_lanes(idx_k_ln_b[l], LANES),
            n_heads, k_col, v_col, TQ)
        iwt = jnp.swapaxes(iwp[:, :, IDX_DIM:IDX_DIM + IDX_HEADS], 1, 2)
        attn = _dsa(q, iq, iwt, iklo, ikhi, k, vt)

        dw = jnp.pad(conv_dw_w[l], ((0, HALO - CONV_WIDTH), (0, 0)))
        u = _conv_module(proj, ua_col, ug_col, dw, conv_dw_b[l][None, :],
                         conv_ln_g[l][None, :], conv_ln_b[l][None, :], TQ)

        y = _mix(u.reshape(m, conv_ch), w_conv_out[l].astype(bf), b_conv_out[l][None, :],
                 proj.reshape(m, -1), ga_col, gc_col, attn.reshape(m, d), tm, 512)
        xf = _mm_res(y, w_out[l].astype(bf), xf, tm, 512)

        t = _ffn_up(xf, norm_ffn_g[l][None, :], w_ffn_gate[l].astype(bf), w_ffn_up[l].astype(bf), tm, 512)
        xf = _mm_res(t, w_ffn_down[l].astype(bf), xf, tm, 512)

    out = _final_norm(xf, final_norm_g[None, :], tm)
    return out.reshape(b, s, d)
```

```python
import functools
import math

import numpy as np
import jax
import jax.numpy as jnp
from jax import lax
from jax.experimental import pallas as pl
from jax.experimental.pallas import tpu as pltpu

HEAD_DIM = 128
N_KV_HEADS = 4
ROPE_DIM = HEAD_DIM // 4
ROPE_THETA = 500000.0
IDX_HEADS = 16
IDX_DIM = 64
IDX_ROPE_DIM = IDX_DIM // 4
TOPK_MAX = 256
CONV_WIDTH = 31
EPS = 1e-6

LANES = 128
SUBLANES = 8
BF16_ROWS = 16
VD = HEAD_DIM + BF16_ROWS
VMEM_LIMIT = 56 * 1024 * 1024

TQ = 256
TK = 512
TKI = 128
CNT_ROWS = 128
CNT_ACC = 32
HALO = 32
MASK_NEG = -1e30
F32_LOWEST = float(np.finfo(np.float32).min)
MAX_BISECT = 200

_NT = (((1,), (1,)), ((), ()))


def _cparams(sem):
    return pltpu.CompilerParams(dimension_semantics=sem, vmem_limit_bytes=VMEM_LIMIT)


def _rms_to_bf16(x, g):
    ms = jnp.mean(x * x, axis=-1, keepdims=True)
    return (x * lax.rsqrt(ms + EPS) * g).astype(jnp.bfloat16)


def _norm_proj_kernel(x_ref, g_ref, w_ref, o_ref, h_scr):
    @pl.when(pl.program_id(1) == 0)
    def _():
        h_scr[...] = _rms_to_bf16(x_ref[...], g_ref[...])

    o_ref[...] = jnp.dot(h_scr[...], w_ref[...],
                         preferred_element_type=jnp.float32).astype(o_ref.dtype)


def _norm_proj(x, g, w, out_dtype, tm, tn):
    m, d = x.shape
    n = w.shape[1]
    return pl.pallas_call(
        _norm_proj_kernel,
        grid=(m // tm, n // tn),
        in_specs=[pl.BlockSpec((tm, d), lambda i, j: (i, 0)),
                  pl.BlockSpec((1, d), lambda i, j: (0, 0)),
                  pl.BlockSpec((d, tn), lambda i, j: (0, j))],
        out_specs=pl.BlockSpec((tm, tn), lambda i, j: (i, j)),
        out_shape=jax.ShapeDtypeStruct((m, n), out_dtype),
        scratch_shapes=[pltpu.VMEM((tm, d), jnp.bfloat16)],
        compiler_params=_cparams(("parallel", "arbitrary")),
        name="norm_proj",
    )(x, g, w)


def _ffn_up_kernel(x_ref, g_ref, wg_ref, wu_ref, o_ref, h_scr):
    @pl.when(pl.program_id(1) == 0)
    def _():
        h_scr[...] = _rms_to_bf16(x_ref[...], g_ref[...])

    h = h_scr[...]
    a = jnp.dot(h, wg_ref[...], preferred_element_type=jnp.float32)
    b = jnp.dot(h, wu_ref[...], preferred_element_type=jnp.float32)
    o_ref[...] = (a * jax.nn.sigmoid(a) * b).astype(o_ref.dtype)


def _ffn_up(x, g, wg, wu, tm, tn):
    m, d = x.shape
    n = wg.shape[1]
    return pl.pallas_call(
        _ffn_up_kernel,
        grid=(m // tm, n // tn),
        in_specs=[pl.BlockSpec((tm, d), lambda i, j: (i, 0)),
                  pl.BlockSpec((1, d), lambda i, j: (0, 0)),
                  pl.BlockSpec((d, tn), lambda i, j: (0, j)),
                  pl.BlockSpec((d, tn), lambda i, j: (0, j))],
        out_specs=pl.BlockSpec((tm, tn), lambda i, j: (i, j)),
        out_shape=jax.ShapeDtypeStruct((m, n), jnp.bfloat16),
        scratch_shapes=[pltpu.VMEM((tm, d), jnp.bfloat16)],
        compiler_params=_cparams(("parallel", "arbitrary")),
        name="ffn_up",
    )(x, g, wg, wu)


def _mm_res_kernel(a_ref, w_ref, r_ref, o_ref):
    o_ref[...] = r_ref[...] + jnp.dot(a_ref[...], w_ref[...],
                                      preferred_element_type=jnp.float32)


def _mm_res(a, w, res, tm, tn):
    m, k = a.shape
    n = w.shape[1]
    return pl.pallas_call(
        _mm_res_kernel,
        grid=(m // tm, n // tn),
        in_specs=[pl.BlockSpec((tm, k), lambda i, j: (i, 0)),
                  pl.BlockSpec((k, tn), lambda i, j: (0, j)),
                  pl.BlockSpec((tm, tn), lambda i, j: (i, j))],
        out_specs=pl.BlockSpec((tm, tn), lambda i, j: (i, j)),
        out_shape=jax.ShapeDtypeStruct((m, n), jnp.float32),
        compiler_params=_cparams(("parallel", "arbitrary")),
        name="mm_res",
    )(a, w, res)


def _mix_kernel(u_ref, w_ref, b_ref, ga_ref, gc_ref, at_ref, y_ref):
    conv = jnp.dot(u_ref[...], w_ref[...], preferred_element_type=jnp.float32) + b_ref[...]
    ga = jax.nn.sigmoid(ga_ref[...].astype(jnp.float32))
    gc = jax.nn.sigmoid(gc_ref[...].astype(jnp.float32))
    y_ref[...] = (ga * at_ref[...].astype(jnp.float32) + gc * conv).astype(y_ref.dtype)


def _mix(u, w, b, proj, ga_col, gc_col, attn, tm, tn):
    m, k = u.shape
    n = w.shape[1]
    ga_blk = ga_col // tn
    gc_blk = gc_col // tn
    return pl.pallas_call(
        _mix_kernel,
        grid=(m // tm, n // tn),
        in_specs=[pl.BlockSpec((tm, k), lambda i, j: (i, 0)),
                  pl.BlockSpec((k, tn), lambda i, j: (0, j)),
                  pl.BlockSpec((1, tn), lambda i, j: (0, j)),
                  pl.BlockSpec((tm, tn), lambda i, j: (i, ga_blk + j)),
                  pl.BlockSpec((tm, tn), lambda i, j: (i, gc_blk + j)),
                  pl.BlockSpec((tm, tn), lambda i, j: (i, j))],
        out_specs=pl.BlockSpec((tm, tn), lambda i, j: (i, j)),
        out_shape=jax.ShapeDtypeStruct((m, n), jnp.bfloat16),
        compiler_params=_cparams(("parallel", "arbitrary")),
        name="gated_mix",
    )(u, w, b, proj, proj, attn)


def _final_norm_kernel(x_ref, g_ref, o_ref):
    x = x_ref[...]
    ms = jnp.mean(x * x, axis=-1, keepdims=True)
    o_ref[...] = x * lax.rsqrt(ms + EPS) * g_ref[...]


def _final_norm(x, g, tm):
    m, d = x.shape
    return pl.pallas_call(
        _final_norm_kernel,
        grid=(m // tm,),
        in_specs=[pl.BlockSpec((tm, d), lambda i: (i, 0)),
                  pl.BlockSpec((1, d), lambda i: (0, 0))],
        out_specs=pl.BlockSpec((tm, d), lambda i: (i, 0)),
        out_shape=jax.ShapeDtypeStruct((m, d), jnp.float32),
        compiler_params=_cparams(("parallel",)),
        name="final_norm",
    )(x, g)


def _rope(x, c, s1, s2, half):
    return (x * c + pltpu.roll(x, LANES - half, axis=1) * s1
            + pltpu.roll(x, half, axis=1) * s2)


def _prep_kernel(q_ref, k_ref, v_ref, f_ref, th_ref, ti_ref, lng_ref, lnb_ref,
                 qo_ref, ko_ref, vto_ref, iqo_ref, iklo_ref, ikhi_ref, iwo_ref,
                 *, n_heads, q_scale, iw_scale):
    ch, s1h, s2h = th_ref[0], th_ref[1], th_ref[2]
    ci, s1i, s2i = ti_ref[0], ti_ref[1], ti_ref[2]
    hh = ROPE_DIM // 2
    hi = IDX_ROPE_DIM // 2
    group = n_heads // N_KV_HEADS
    tm = q_ref.shape[0]
    for h in range(n_heads):
        xh = q_ref[:, h * HEAD_DIM:(h + 1) * HEAD_DIM].astype(jnp.float32)
        j = h % group
        qo_ref[h // group, j * tm:(j + 1) * tm, :] = (
            _rope(xh, ch, s1h, s2h, hh) * q_scale).astype(qo_ref.dtype)
    for g in range(N_KV_HEADS):
        xk = k_ref[:, g * HEAD_DIM:(g + 1) * HEAD_DIM].astype(jnp.float32)
        ko_ref[g] = _rope(xk, ch, s1h, s2h, hh).astype(ko_ref.dtype)
        xv = v_ref[:, g * HEAD_DIM:(g + 1) * HEAD_DIM].astype(jnp.float32)
        vto_ref[g, 0:HEAD_DIM, :] = xv.T.astype(vto_ref.dtype)
        vto_ref[g, HEAD_DIM:VD, :] = jnp.ones((BF16_ROWS, tm), vto_ref.dtype)
    n_pairs = IDX_HEADS * IDX_DIM // LANES
    for p in range(n_pairs):
        xi = f_ref[:, p * LANES:(p + 1) * LANES]
        iqo_ref[:, p * LANES:(p + 1) * LANES] = _rope(xi, ci, s1i, s2i, hi).astype(iqo_ref.dtype)
    xt = f_ref[:, n_pairs * LANES:(n_pairs + 1) * LANES]
    lane = lax.broadcasted_iota(jnp.int32, xt.shape, 1)
    is_ik = lane < IDX_DIM
    mu = jnp.sum(jnp.where(is_ik, xt, 0.0), axis=-1, keepdims=True) * (1.0 / IDX_DIM)
    xc = jnp.where(is_ik, xt - mu, 0.0)
    var = jnp.sum(xc * xc, axis=-1, keepdims=True) * (1.0 / IDX_DIM)
    y = xc * lax.rsqrt(var + EPS) * lng_ref[...] + lnb_ref[...]
    r = _rope(y, ci, s1i, s2i, hi)
    iklo_ref[...] = r.astype(iklo_ref.dtype)
    ikhi_ref[...] = pltpu.roll(r, IDX_DIM, axis=1).astype(ikhi_ref.dtype)
    iwo_ref[...] = xt * iw_scale


def _prep(proj, fproj, tab_h, tab_i, lng, lnb, n_heads, k_col, v_col, tm):
    b, s, _ = proj.shape
    d = n_heads * HEAD_DIM
    group = n_heads // N_KV_HEADS
    kvw = N_KV_HEADS * HEAD_DIM
    fw = fproj.shape[-1]
    iqw = IDX_HEADS * IDX_DIM
    kern = functools.partial(
        _prep_kernel, n_heads=n_heads,
        q_scale=(HEAD_DIM ** -0.5) * math.log2(math.e),
        iw_scale=(IDX_HEADS ** -0.5) * (IDX_DIM ** -0.5))
    bf = jnp.bfloat16
    return pl.pallas_call(
        kern,
        grid=(b, s // tm),
        in_specs=[pl.BlockSpec((None, tm, d), lambda bi, i: (bi, i, 0)),
                  pl.BlockSpec((None, tm, kvw), lambda bi, i: (bi, i, k_col // kvw)),
                  pl.BlockSpec((None, tm, kvw), lambda bi, i: (bi, i, v_col // kvw)),
                  pl.BlockSpec((None, tm, fw), lambda bi, i: (bi, i, 0)),
                  pl.BlockSpec((None, 3, tm, LANES), lambda bi, i: (bi, 0, i, 0)),
                  pl.BlockSpec((None, 3, tm, LANES), lambda bi, i: (bi, 0, i, 0)),
                  pl.BlockSpec((1, LANES), lambda bi, i: (0, 0)),
                  pl.BlockSpec((1, LANES), lambda bi, i: (0, 0))],
        out_specs=[pl.BlockSpec((None, N_KV_HEADS, None, group * tm, HEAD_DIM),
                                lambda bi, i: (bi, 0, i, 0, 0)),
                   pl.BlockSpec((None, N_KV_HEADS, tm, HEAD_DIM), lambda bi, i: (bi, 0, i, 0)),
                   pl.BlockSpec((None, N_KV_HEADS, VD, tm), lambda bi, i: (bi, 0, 0, i)),
                   pl.BlockSpec((None, tm, iqw), lambda bi, i: (bi, i, 0)),
                   pl.BlockSpec((None, tm, LANES), lambda bi, i: (bi, i, 0)),
                   pl.BlockSpec((None, tm, LANES), lambda bi, i: (bi, i, 0)),
                   pl.BlockSpec((None, tm, LANES), lambda bi, i: (bi, i, 0))],
        out_shape=[jax.ShapeDtypeStruct((b, N_KV_HEADS, s // tm, group * tm, HEAD_DIM), bf),
                   jax.ShapeDtypeStruct((b, N_KV_HEADS, s, HEAD_DIM), bf),
                   jax.ShapeDtypeStruct((b, N_KV_HEADS, VD, s), bf),
                   jax.ShapeDtypeStruct((b, s, iqw), bf),
                   jax.ShapeDtypeStruct((b, s, LANES), bf),
                   jax.ShapeDtypeStruct((b, s, LANES), bf),
                   jax.ShapeDtypeStruct((b, s, LANES), jnp.float32)],
        compiler_params=_cparams(("parallel", "parallel")),
        name="rope_prep",
    )(proj, proj, proj, fproj, tab_h, tab_i, lng, lnb)


def _dsa_kernel(bi_ref, qi_ref, kt_ref,
                q_ref, iq_ref, iwt_ref, iklo_ref, ikhi_ref, k_ref, vt_ref,
                o_ref,
                sc_ref, bias_ref, thr_ref, m_ref, acc_ref,
                *, n_heads, topk):
    t = pl.program_id(0)
    qi = qi_ref[t]
    kt = kt_ref[t]
    n_keys = (qi + 1) * TQ
    n_kt = (n_keys + TK - 1) // TK
    group = n_heads // N_KV_HEADS
    f32 = jnp.float32

    def score_chunk(r0):
        lo = iklo_ref[pl.ds(r0, TKI), :]
        hi = ikhi_ref[pl.ds(r0, TKI), :]
        acc = jnp.zeros((TKI, TQ), f32)
        for p in range(IDX_HEADS // 2):
            w = iq_ref[:, p * LANES:(p + 1) * LANES]
            ze = lax.dot_general(lo, w, _NT, preferred_element_type=f32)
            zo = lax.dot_general(hi, w, _NT, preferred_element_type=f32)
            acc = acc + iwt_ref[2 * p:2 * p + 1, :] * jnp.maximum(ze, 0.0)
            acc = acc + iwt_ref[2 * p + 1:2 * p + 2, :] * jnp.maximum(zo, 0.0)
        return acc

    def fold8(x, op):
        return op(x.reshape(x.shape[0] // SUBLANES, SUBLANES, x.shape[1]), axis=0)

    @pl.when(kt == 0)
    def _index_and_select():
        def full_block(c, carry):
            mx, mn = carry
            for half in range(TQ // TKI):
                r0 = pl.multiple_of(c * TQ + half * TKI, TKI)
                sc = score_chunk(r0)
                sc_ref[pl.ds(r0, TKI), :] = sc
                mx = jnp.maximum(mx, fold8(sc, jnp.max))
                mn = jnp.minimum(mn, fold8(sc, jnp.min))
            return mx, mn

        mx0 = jnp.full((SUBLANES, TQ), -jnp.inf, f32)
        mn0 = jnp.full((SUBLANES, TQ), jnp.inf, f32)
        mx, mn = lax.fori_loop(0, qi, full_block, (mx0, mn0))
        for dchunk in range(TQ // TKI):
            r0 = pl.multiple_of(qi * TQ + dchunk * TKI, TKI)
            sc = score_chunk(r0)
            row = lax.broadcasted_iota(jnp.int32, (TKI, TQ), 0) + dchunk * TKI
            col = lax.broadcasted_iota(jnp.int32, (TKI, TQ), 1)
            valid = row <= col
            sc_ref[pl.ds(r0, TKI), :] = jnp.where(valid, sc, -jnp.inf)
            mx = jnp.maximum(mx, fold8(jnp.where(valid, sc, -jnp.inf), jnp.max))
            mn = jnp.minimum(mn, fold8(jnp.where(valid, sc, jnp.inf), jnp.min))

        @pl.when(n_kt * TK > n_keys)
        def _():
            sc_ref[pl.ds(pl.multiple_of(n_keys, TQ), TQ), :] = jnp.full((TQ, TQ), -jnp.inf, f32)

        hi0 = jnp.max(mx, axis=0, keepdims=True)
        lo0 = jnp.min(mn, axis=0, keepdims=True)
        lane = lax.broadcasted_iota(jnp.int32, (1, TQ), 1)
        take_all = (qi * TQ + lane + 1) <= topk
        kf = f32(topk)
        n_cnt = n_keys // CNT_ROWS

        def count_ge(mid):
            def body(c, cnt):
                r0 = pl.multiple_of(c * CNT_ROWS, CNT_ROWS)
                ind = jnp.where(sc_ref[pl.ds(r0, CNT_ROWS), :] >= mid, f32(1), f32(0))
                return cnt + jnp.sum(ind.reshape(CNT_ROWS // CNT_ACC, CNT_ACC, TQ), axis=0)
            cnt = lax.fori_loop(0, n_cnt, body, jnp.zeros((CNT_ACC, TQ), f32))
            return jnp.sum(cnt, axis=0, keepdims=True)

        def cond(st):
            return jnp.logical_and(st[0] < MAX_BISECT, st[1] > 0)

        def body(st):
            it, _, lo, hi, thr, done = st
            mid = lo * 0.5 + hi * 0.5
            stuck = jnp.logical_or(mid <= lo, mid >= hi)
            cnt = count_ge(mid)
            exact = cnt == kf
            ge = cnt >= kf
            act = done < 0.5
            thr = jnp.where(jnp.logical_and(act, exact), mid,
                            jnp.where(jnp.logical_and(act, stuck), lo, thr))
            newly = jnp.logical_and(act, jnp.logical_or(exact, stuck))
            move = jnp.logical_and(act, jnp.logical_not(newly))
            lo = jnp.where(jnp.logical_and(move, ge), mid, lo)
            hi = jnp.where(jnp.logical_and(move, jnp.logical_not(ge)), mid, hi)
            done = jnp.where(newly, f32(1), done)
            n_act = jnp.sum(jnp.where(done < 0.5, jnp.int32(1), jnp.int32(0)))
            return it + 1, n_act, lo, hi, thr, done

        done0 = jnp.where(take_all, f32(1), f32(0))
        thr0 = jnp.full((1, TQ), F32_LOWEST, f32)
        n_act0 = jnp.sum(jnp.where(take_all, jnp.int32(0), jnp.int32(1)))
        st = lax.while_loop(cond, body, (jnp.int32(0), n_act0, lo0, hi0, thr0, done0))
        thr_ref[...] = st[4]

        m_ref[...] = jnp.full(m_ref.shape, MASK_NEG, f32)
        acc_ref[...] = jnp.zeros(acc_ref.shape, f32)

    k0 = pl.multiple_of(kt * TK, TK)
    bias_ref[...] = jnp.where(sc_ref[pl.ds(k0, TK), :] >= thr_ref[...], f32(0), f32(MASK_NEG))

    def logits(g):
        s = lax.dot_general(k_ref[g], q_ref[g], _NT, preferred_element_type=f32)
        return s + jnp.tile(bias_ref[...], (1, group))

    def softmax_pv(g, s):
        m_prev = m_ref[g]
        m_new = jnp.maximum(m_prev, jnp.max(s, axis=0, keepdims=True))
        alpha = jnp.exp2(m_prev - m_new)
        p = jnp.exp2(s - m_new).astype(jnp.bfloat16)
        pv = jnp.dot(vt_ref[g], p, preferred_element_type=f32)
        acc_ref[g] = alpha * acc_ref[g] + pv
        m_ref[g] = m_new

    s_next = logits(0)
    for g in range(N_KV_HEADS):
        s_cur = s_next
        if g + 1 < N_KV_HEADS:
            s_next = logits(g + 1)
        softmax_pv(g, s_cur)

    @pl.when(kt == n_kt - 1)
    def _finalize():
        for g in range(N_KV_HEADS):
            a = acc_ref[g]
            o = a[:HEAD_DIM] * (1.0 / a[HEAD_DIM:HEAD_DIM + 1])
            for j in range(group):
                h = g * group + j
                o_ref[:, h * HEAD_DIM:(h + 1) * HEAD_DIM] = o[:, j * TQ:(j + 1) * TQ].T.astype(o_ref.dtype)


def _dsa_steps(b, s):
    bi, qi, kt = [], [], []
    for bb in range(b):
        for q in range(s // TQ):
            for k in range(-(-((q + 1) * TQ) // TK)):
                bi.append(bb)
                qi.append(q)
                kt.append(k)
    return (np.asarray(bi, np.int32), np.asarray(qi, np.int32), np.asarray(kt, np.int32))


def _dsa(q, iq, iwt, iklo, ikhi, k, vt):
    b, _, nq, gtq, _ = q.shape
    group = gtq // TQ
    n_heads = N_KV_HEADS * group
    s = nq * TQ
    assert TK == 2 * TQ and TQ % TKI == 0 and s % TK == 0
    topk = min(TOPK_MAX, s // 4)
    bi, qi, kt = _dsa_steps(b, s)
    iqw = iq.shape[-1]
    kern = functools.partial(_dsa_kernel, n_heads=n_heads, topk=topk)
    grid_spec = pltpu.PrefetchScalarGridSpec(
        num_scalar_prefetch=3,
        grid=(len(bi),),
        in_specs=[
            pl.BlockSpec((None, N_KV_HEADS, None, gtq, HEAD_DIM),
                         lambda t, b_, q_, k_: (b_[t], 0, q_[t], 0, 0)),
            pl.BlockSpec((None, TQ, iqw), lambda t, b_, q_, k_: (b_[t], q_[t], 0)),
            pl.BlockSpec((None, IDX_HEADS, TQ), lambda t, b_, q_, k_: (b_[t], 0, q_[t])),
            pl.BlockSpec((None, s, LANES), lambda t, b_, q_, k_: (b_[t], 0, 0)),
            pl.BlockSpec((None, s, LANES), lambda t, b_, q_, k_: (b_[t], 0, 0)),
            pl.BlockSpec((None, N_KV_HEADS, TK, HEAD_DIM), lambda t, b_, q_, k_: (b_[t], 0, k_[t], 0)),
            pl.BlockSpec((None, N_KV_HEADS, VD, TK), lambda t, b_, q_, k_: (b_[t], 0, 0, k_[t])),
        ],
        out_specs=pl.BlockSpec((None, TQ, n_heads * HEAD_DIM), lambda t, b_, q_, k_: (b_[t], q_[t], 0)),
        scratch_shapes=[
            pltpu.VMEM((s, TQ), jnp.float32),
            pltpu.VMEM((TK, TQ), jnp.float32),
            pltpu.VMEM((1, TQ), jnp.float32),
            pltpu.VMEM((N_KV_HEADS, 1, gtq), jnp.float32),
            pltpu.VMEM((N_KV_HEADS, VD, gtq), jnp.float32),
        ],
    )
    return pl.pallas_call(
        kern,
        grid_spec=grid_spec,
        out_shape=jax.ShapeDtypeStruct((b, s, n_heads * HEAD_DIM), jnp.bfloat16),
        compiler_params=_cparams(("arbitrary",)),
        name="dsa_attention",
    )(jnp.asarray(bi), jnp.asarray(qi), jnp.asarray(kt), q, iq, iwt, iklo, ikhi, k, vt)


CONV_ROWS = 32
CONV_COLS = 512


def _conv_kernel(ua_ref, ug_ref, ha_ref, hg_ref, w_ref, b_ref, g_ref, beta_ref, o_ref,
                 glu_scr, y_scr, *, ts):
    i = pl.program_id(1)
    f32 = jnp.float32
    ua = ua_ref[...].astype(f32)
    ug = ug_ref[...].astype(f32)
    glu_scr[HALO:HALO + ts, :] = ua * jax.nn.sigmoid(ug)
    ha = ha_ref[...].astype(f32)
    hg = hg_ref[...].astype(f32)
    halo = ha * jax.nn.sigmoid(hg)
    glu_scr[0:HALO, :] = jnp.where(i > 0, halo, 0.0)

    c = y_scr.shape[1]
    off = HALO - (CONV_WIDTH - 1)
    for cs in range(0, c, CONV_COLS):
        wcs = w_ref[:, cs:cs + CONV_COLS]
        for r0 in range(0, ts, CONV_ROWS):
            acc = jnp.zeros((CONV_ROWS, CONV_COLS), f32) + b_ref[:, cs:cs + CONV_COLS]
            for j in range(CONV_WIDTH):
                acc = acc + wcs[j:j + 1, :] * glu_scr[r0 + off + j:r0 + off + j + CONV_ROWS, cs:cs + CONV_COLS]
            y_scr[r0:r0 + CONV_ROWS, cs:cs + CONV_COLS] = acc

    y = y_scr[...]
    mu = jnp.mean(y, axis=-1, keepdims=True)
    yc = y - mu
    var = jnp.mean(yc * yc, axis=-1, keepdims=True)
    z = yc * lax.rsqrt(var + EPS) * g_ref[...] + beta_ref[...]
    o_ref[...] = (z * jax.nn.sigmoid(z)).astype(o_ref.dtype)


def _conv_module(proj, ua_col, ug_col, w, bias, g, beta, ts):
    b, s, _ = proj.shape
    c = w.shape[1]
    hb = ts // HALO
    kern = functools.partial(_conv_kernel, ts=ts)
    return pl.pallas_call(
        kern,
        grid=(b, s // ts),
        in_specs=[pl.BlockSpec((None, ts, c), lambda bi, i: (bi, i, ua_col // c)),
                  pl.BlockSpec((None, ts, c), lambda bi, i: (bi, i, ug_col // c)),
                  pl.BlockSpec((None, HALO, c), lambda bi, i: (bi, jnp.maximum(i * hb - 1, 0), ua_col // c)),
                  pl.BlockSpec((None, HALO, c), lambda bi, i: (bi, jnp.maximum(i * hb - 1, 0), ug_col // c)),
                  pl.BlockSpec((HALO, c), lambda bi, i: (0, 0)),
                  pl.BlockSpec((1, c), lambda bi, i: (0, 0)),
                  pl.BlockSpec((1, c), lambda bi, i: (0, 0)),
                  pl.BlockSpec((1, c), lambda bi, i: (0, 0))],
        out_specs=pl.BlockSpec((None, ts, c), lambda bi, i: (bi, i, 0)),
        out_shape=jax.ShapeDtypeStruct((b, s, c), jnp.bfloat16),
        scratch_shapes=[pltpu.VMEM((HALO + ts, c), jnp.float32),
                        pltpu.VMEM((ts, c), jnp.float32)],
        compiler_params=_cparams(("parallel", "arbitrary")),
        name="conv_module",
    )(proj, proj, proj, proj, w, bias, g, beta)


def _rope_tables(pos, rot_dim, period):
    half = rot_dim // 2
    freqs = ROPE_THETA ** (-jnp.arange(half, dtype=jnp.float32) / half)
    ang = pos.astype(jnp.float32)[..., None] * freqs
    cos, sin = jnp.cos(ang), jnp.sin(ang)
    pad = period - rot_dim
    z = jnp.zeros(ang.shape[:2] + (pad,), jnp.float32)
    zh = jnp.zeros_like(sin)
    c = jnp.concatenate([cos, cos, z + 1.0], axis=-1)
    s1 = jnp.concatenate([-sin, zh, z], axis=-1)
    s2 = jnp.concatenate([zh, sin, z], axis=-1)
    tab = jnp.stack([c, s1, s2], axis=1)
    return jnp.tile(tab, (1, 1, 1, LANES // period))


def _pad_lanes(v, width):
    return jnp.pad(v, (0, width - v.shape[0]))[None, :].astype(jnp.float32)


def kernel(x, positions, norm_mix_g, w_in, idx_k_ln_g, idx_k_ln_b, conv_dw_w, conv_dw_b, conv_ln_g, conv_ln_b, w_conv_out, b_conv_out, w_out, norm_ffn_g, w_ffn_gate, w_ffn_up, w_ffn_down, final_norm_g):
    b, s, d = x.shape
    depth = w_in.shape[0]
    n_heads = d // HEAD_DIM
    kvw = N_KV_HEADS * HEAD_DIM
    iqw = IDX_HEADS * IDX_DIM
    conv_ch = conv_dw_w.shape[-1]
    m = b * s
    bf = jnp.bfloat16

    c_q1 = d
    c_k1 = c_q1 + kvw
    c_v1 = c_k1 + kvw
    c_iq1 = c_v1 + iqw
    c_ik1 = c_iq1 + IDX_DIM
    c_iw1 = c_ik1 + IDX_HEADS
    c_glu1 = c_iw1 + 2 * conv_ch
    c_gate1 = c_glu1 + 2 * d
    ua_col, ug_col = d, d + conv_ch
    ga_col = d + 2 * conv_ch
    gc_col = ga_col + d
    k_col = gc_col + d
    v_col = k_col + kvw

    tab_h = _rope_tables(positions, ROPE_DIM, HEAD_DIM)
    tab_i = _rope_tables(positions, IDX_ROPE_DIM, IDX_DIM)

    tm = min(1024, m)
    xf = x.reshape(m, d)
    for l in range(depth):
        wl = w_in[l]
        w_slab = jnp.concatenate(
            [wl[:, :c_q1], wl[:, c_iw1:c_glu1], wl[:, c_glu1:c_gate1], wl[:, c_q1:c_v1]],
            axis=1).astype(bf)
        w_idx = jnp.pad(wl[:, c_v1:c_iw1], ((0, 0), (0, LANES - IDX_DIM - IDX_HEADS))).astype(bf)
        g_mix = norm_mix_g[l][None, :]

        proj = _norm_proj(xf, g_mix, w_slab, bf, tm, 512).reshape(b, s, -1)
        fproj = _norm_proj(xf, g_mix, w_idx, jnp.float32, tm, w_idx.shape[1]).reshape(b, s, -1)

        q, k, vt, iq, iklo, ikhi, iwp = _prep(
            proj, fproj, tab_h, tab_i,
            _pad_lanes(idx_k_ln_g[l], LANES), _pad_lanes(idx_k_ln_b[l], LANES),
            n_heads, k_col, v_col, TQ)
        iwt = jnp.swapaxes(iwp[:, :, IDX_DIM:IDX_DIM + IDX_HEADS], 1, 2)
        attn = _dsa(q, iq, iwt, iklo, ikhi, k, vt)

        dw = jnp.pad(conv_dw_w[l], ((0, HALO - CONV_WIDTH), (0, 0)))
        u = _conv_module(proj, ua_col, ug_col, dw, conv_dw_b[l][None, :],
                         conv_ln_g[l][None, :], conv_ln_b[l][None, :], TQ)

        y = _mix(u.reshape(m, conv_ch), w_conv_out[l].astype(bf), b_conv_out[l][None, :],
                 proj.reshape(m, -1), ga_col, gc_col, attn.reshape(m, d), tm, 512)
        xf = _mm_res(y, w_out[l].astype(bf), xf, tm, 512)

        t = _ffn_up(xf, norm_ffn_g[l][None, :], w_ffn_gate[l].astype(bf), w_ffn_up[l].astype(bf), tm, 512)
        xf = _mm_res(t, w_ffn_down[l].astype(bf), xf, tm, 512)

    out = _final_norm(xf, final_norm_g[None, :], tm)
    return out.reshape(b, s, d)
```

```python
import functools
import math

import numpy as np
import jax
import jax.numpy as jnp
from jax import lax
from jax.experimental import pallas as pl
from jax.experimental.pallas import tpu as pltpu

HEAD_DIM = 128
N_KV_HEADS = 4
ROPE_DIM = HEAD_DIM // 4
ROPE_THETA = 500000.0
IDX_HEADS = 16
IDX_DIM = 64
IDX_ROPE_DIM = IDX_DIM // 4
TOPK_MAX = 256
CONV_WIDTH = 31
EPS = 1e-6

LANES = 128
SUBLANES = 8
BF16_ROWS = 16
VD = HEAD_DIM + BF16_ROWS
VMEM_LIMIT = 56 * 1024 * 1024

TQ = 256
TK = 1024
TKI = 128
UNIT_HEADS = 2
CNT_ROWS = 256
CNT_ACC = 32
HALO = 32
MASK_NEG = -1e30
DEN_LIMIT = 2.0 ** 64
F32_LOWEST = float(np.finfo(np.float32).min)
MAX_BISECT = 200

_NT = (((1,), (1,)), ((), ()))


def _cparams(sem):
    return pltpu.CompilerParams(dimension_semantics=sem, vmem_limit_bytes=VMEM_LIMIT)


def _rms_to_bf16(x, g):
    ms = jnp.mean(x * x, axis=-1, keepdims=True)
    return (x * lax.rsqrt(ms + EPS) * g).astype(jnp.bfloat16)


def _norm_proj_kernel(x_ref, g_ref, w_ref, o_ref, h_scr):
    @pl.when(pl.program_id(1) == 0)
    def _():
        h_scr[...] = _rms_to_bf16(x_ref[...], g_ref[...])

    o_ref[...] = jnp.dot(h_scr[...], w_ref[...],
                         preferred_element_type=jnp.float32).astype(o_ref.dtype)


def _norm_proj(x, g, w, out_dtype, tm, tn):
    m, d = x.shape
    n = w.shape[1]
    return pl.pallas_call(
        _norm_proj_kernel,
        grid=(m // tm, n // tn),
        in_specs=[pl.BlockSpec((tm, d), lambda i, j: (i, 0)),
                  pl.BlockSpec((1, d), lambda i, j: (0, 0)),
                  pl.BlockSpec((d, tn), lambda i, j: (0, j))],
        out_specs=pl.BlockSpec((tm, tn), lambda i, j: (i, j)),
        out_shape=jax.ShapeDtypeStruct((m, n), out_dtype),
        scratch_shapes=[pltpu.VMEM((tm, d), jnp.bfloat16)],
        compiler_params=_cparams(("parallel", "arbitrary")),
        name="norm_proj",
    )(x, g, w)


def _ffn_up_kernel(x_ref, g_ref, wg_ref, wu_ref, o_ref, h_scr):
    @pl.when(pl.program_id(1) == 0)
    def _():
        h_scr[...] = _rms_to_bf16(x_ref[...], g_ref[...])

    h = h_scr[...]
    a = jnp.dot(h, wg_ref[...], preferred_element_type=jnp.float32)
    b = jnp.dot(h, wu_ref[...], preferred_element_type=jnp.float32)
    o_ref[...] = (a * jax.nn.sigmoid(a) * b).astype(o_ref.dtype)


def _ffn_up(x, g, wg, wu, tm, tn):
    m, d = x.shape
    n = wg.shape[1]
    return pl.pallas_call(
        _ffn_up_kernel,
        grid=(m // tm, n // tn),
        in_specs=[pl.BlockSpec((tm, d), lambda i, j: (i, 0)),
                  pl.BlockSpec((1, d), lambda i, j: (0, 0)),
                  pl.BlockSpec((d, tn), lambda i, j: (0, j)),
                  pl.BlockSpec((d, tn), lambda i, j: (0, j))],
        out_specs=pl.BlockSpec((tm, tn), lambda i, j: (i, j)),
        out_shape=jax.ShapeDtypeStruct((m, n), jnp.bfloat16),
        scratch_shapes=[pltpu.VMEM((tm, d), jnp.bfloat16)],
        compiler_params=_cparams(("parallel", "arbitrary")),
        name="ffn_up",
    )(x, g, wg, wu)


def _mm_res_kernel(a_ref, w_ref, r_ref, o_ref):
    o_ref[...] = r_ref[...] + jnp.dot(a_ref[...], w_ref[...],
                                      preferred_element_type=jnp.float32)


def _mm_res(a, w, res, tm, tn):
    m, k = a.shape
    n = w.shape[1]
    return pl.pallas_call(
        _mm_res_kernel,
        grid=(m // tm, n // tn),
        in_specs=[pl.BlockSpec((tm, k), lambda i, j: (i, 0)),
                  pl.BlockSpec((k, tn), lambda i, j: (0, j)),
                  pl.BlockSpec((tm, tn), lambda i, j: (i, j))],
        out_specs=pl.BlockSpec((tm, tn), lambda i, j: (i, j)),
        out_shape=jax.ShapeDtypeStruct((m, n), jnp.float32),
        compiler_params=_cparams(("parallel", "arbitrary")),
        name="mm_res",
    )(a, w, res)


def _mix_kernel(u_ref, w_ref, b_ref, ga_ref, gc_ref, at_ref, y_ref):
    conv = jnp.dot(u_ref[...], w_ref[...], preferred_element_type=jnp.float32) + b_ref[...]
    ga = jax.nn.sigmoid(ga_ref[...].astype(jnp.float32))
    gc = jax.nn.sigmoid(gc_ref[...].astype(jnp.float32))
    y_ref[...] = (ga * at_ref[...].astype(jnp.float32) + gc * conv).astype(y_ref.dtype)


def _mix(u, w, b, proj, ga_col, gc_col, attn, tm, tn):
    m, k = u.shape
    n = w.shape[1]
    ga_blk = ga_col // tn
    gc_blk = gc_col // tn
    return pl.pallas_call(
        _mix_kernel,
        grid=(m // tm, n // tn),
        in_specs=[pl.BlockSpec((tm, k), lambda i, j: (i, 0)),
                  pl.BlockSpec((k, tn), lambda i, j: (0, j)),
                  pl.BlockSpec((1, tn), lambda i, j: (0, j)),
                  pl.BlockSpec((tm, tn), lambda i, j: (i, ga_blk + j)),
                  pl.BlockSpec((tm, tn), lambda i, j: (i, gc_blk + j)),
                  pl.BlockSpec((tm, tn), lambda i, j: (i, j))],
        out_specs=pl.BlockSpec((tm, tn), lambda i, j: (i, j)),
        out_shape=jax.ShapeDtypeStruct((m, n), jnp.bfloat16),
        compiler_params=_cparams(("parallel", "arbitrary")),
        name="gated_mix",
    )(u, w, b, proj, proj, attn)


def _final_norm_kernel(x_ref, g_ref, o_ref):
    x = x_ref[...]
    ms = jnp.mean(x * x, axis=-1, keepdims=True)
    o_ref[...] = x * lax.rsqrt(ms + EPS) * g_ref[...]


def _final_norm(x, g, tm):
    m, d = x.shape
    return pl.pallas_call(
        _final_norm_kernel,
        grid=(m // tm,),
        in_specs=[pl.BlockSpec((tm, d), lambda i: (i, 0)),
                  pl.BlockSpec((1, d), lambda i: (0, 0))],
        out_specs=pl.BlockSpec((tm, d), lambda i: (i, 0)),
        out_shape=jax.ShapeDtypeStruct((m, d), jnp.float32),
        compiler_params=_cparams(("parallel",)),
        name="final_norm",
    )(x, g)


def _rope(x, c, s1, s2, half):
    return (x * c + pltpu.roll(x, LANES - half, axis=1) * s1
            + pltpu.roll(x, half, axis=1) * s2)


def _prep_kernel(q_ref, k_ref, v_ref, f_ref, th_ref, ti_ref, lng_ref, lnb_ref,
                 qo_ref, ko_ref, vto_ref, iqo_ref, iklo_ref, ikhi_ref, iwo_ref,
                 *, n_heads, q_scale, iw_scale):
    ch, s1h, s2h = th_ref[0], th_ref[1], th_ref[2]
    ci, s1i, s2i = ti_ref[0], ti_ref[1], ti_ref[2]
    hh = ROPE_DIM // 2
    hi = IDX_ROPE_DIM // 2
    group = n_heads // N_KV_HEADS
    tm = q_ref.shape[0]
    for h in range(n_heads):
        xh = q_ref[:, h * HEAD_DIM:(h + 1) * HEAD_DIM].astype(jnp.float32)
        j = h % group
        qo_ref[h // group, j * tm:(j + 1) * tm, :] = (
            _rope(xh, ch, s1h, s2h, hh) * q_scale).astype(qo_ref.dtype)
    for g in range(N_KV_HEADS):
        xk = k_ref[:, g * HEAD_DIM:(g + 1) * HEAD_DIM].astype(jnp.float32)
        ko_ref[g] = _rope(xk, ch, s1h, s2h, hh).astype(ko_ref.dtype)
        xv = v_ref[:, g * HEAD_DIM:(g + 1) * HEAD_DIM].astype(jnp.float32)
        vto_ref[g, 0:HEAD_DIM, :] = xv.T.astype(vto_ref.dtype)
        vto_ref[g, HEAD_DIM:VD, :] = jnp.ones((BF16_ROWS, tm), vto_ref.dtype)
    n_pairs = IDX_HEADS * IDX_DIM // LANES
    for p in range(n_pairs):
        xi = f_ref[:, p * LANES:(p + 1) * LANES]
        iqo_ref[:, p * LANES:(p + 1) * LANES] = _rope(xi, ci, s1i, s2i, hi).astype(iqo_ref.dtype)
    xt = f_ref[:, n_pairs * LANES:(n_pairs + 1) * LANES]
    lane = lax.broadcasted_iota(jnp.int32, xt.shape, 1)
    is_ik = lane < IDX_DIM
    mu = jnp.sum(jnp.where(is_ik, xt, 0.0), axis=-1, keepdims=True) * (1.0 / IDX_DIM)
    xc = jnp.where(is_ik, xt - mu, 0.0)
    var = jnp.sum(xc * xc, axis=-1, keepdims=True) * (1.0 / IDX_DIM)
    y = xc * lax.rsqrt(var + EPS) * lng_ref[...] + lnb_ref[...]
    r = _rope(y, ci, s1i, s2i, hi)
    iklo_ref[...] = r.astype(iklo_ref.dtype)
    ikhi_ref[...] = pltpu.roll(r, IDX_DIM, axis=1).astype(ikhi_ref.dtype)
    iwo_ref[...] = xt * iw_scale


def _prep(proj, fproj, tab_h, tab_i, lng, lnb, n_heads, k_col, v_col, tm):
    b, s, _ = proj.shape
    d = n_heads * HEAD_DIM
    group = n_heads // N_KV_HEADS
    kvw = N_KV_HEADS * HEAD_DIM
    fw = fproj.shape[-1]
    iqw = IDX_HEADS * IDX_DIM
    kern = functools.partial(
        _prep_kernel, n_heads=n_heads,
        q_scale=(HEAD_DIM ** -0.5) * math.log2(math.e),
        iw_scale=(IDX_HEADS ** -0.5) * (IDX_DIM ** -0.5))
    bf = jnp.bfloat16
    return pl.pallas_call(
        kern,
        grid=(b, s // tm),
        in_specs=[pl.BlockSpec((None, tm, d), lambda bi, i: (bi, i, 0)),
                  pl.BlockSpec((None, tm, kvw), lambda bi, i: (bi, i, k_col // kvw)),
                  pl.BlockSpec((None, tm, kvw), lambda bi, i: (bi, i, v_col // kvw)),
                  pl.BlockSpec((None, tm, fw), lambda bi, i: (bi, i, 0)),
                  pl.BlockSpec((None, 3, tm, LANES), lambda bi, i: (bi, 0, i, 0)),
                  pl.BlockSpec((None, 3, tm, LANES), lambda bi, i: (bi, 0, i, 0)),
                  pl.BlockSpec((1, LANES), lambda bi, i: (0, 0)),
                  pl.BlockSpec((1, LANES), lambda bi, i: (0, 0))],
        out_specs=[pl.BlockSpec((None, N_KV_HEADS, None, group * tm, HEAD_DIM),
                                lambda bi, i: (bi, 0, i, 0, 0)),
                   pl.BlockSpec((None, N_KV_HEADS, tm, HEAD_DIM), lambda bi, i: (bi, 0, i, 0)),
                   pl.BlockSpec((None, N_KV_HEADS, VD, tm), lambda bi, i: (bi, 0, 0, i)),
                   pl.BlockSpec((None, tm, iqw), lambda bi, i: (bi, i, 0)),
                   pl.BlockSpec((None, tm, LANES), lambda bi, i: (bi, i, 0)),
                   pl.BlockSpec((None, tm, LANES), lambda bi, i: (bi, i, 0)),
                   pl.BlockSpec((None, tm, LANES), lambda bi, i: (bi, i, 0))],
        out_shape=[jax.ShapeDtypeStruct((b, N_KV_HEADS, s // tm, group * tm, HEAD_DIM), bf),
                   jax.ShapeDtypeStruct((b, N_KV_HEADS, s, HEAD_DIM), bf),
                   jax.ShapeDtypeStruct((b, N_KV_HEADS, VD, s), bf),
                   jax.ShapeDtypeStruct((b, s, iqw), bf),
                   jax.ShapeDtypeStruct((b, s, LANES), bf),
                   jax.ShapeDtypeStruct((b, s, LANES), bf),
                   jax.ShapeDtypeStruct((b, s, LANES), jnp.float32)],
        compiler_params=_cparams(("parallel", "parallel")),
        name="rope_prep",
    )(proj, proj, proj, fproj, tab_h, tab_i, lng, lnb)


def _dsa_kernel(bi_ref, qi_ref, kt_ref,
                q_ref, iq_ref, iwt_ref, iklo_ref, ikhi_ref, k_ref, vt_ref,
                o_ref,
                sc_ref, bias_ref, thr_ref, m_ref, acc_ref, redo_ref, flag_ref,
                *, n_heads, topk):
    t = pl.program_id(0)
    qi = qi_ref[t]
    kt = kt_ref[t]
    n_keys = (qi + 1) * TQ
    n_kt = (n_keys + TK - 1) // TK
    group = n_heads // N_KV_HEADS
    f32 = jnp.float32

    def score_chunk(r0):
        lo = iklo_ref[pl.ds(r0, TKI), :]
        hi = ikhi_ref[pl.ds(r0, TKI), :]
        acc = jnp.zeros((TKI, TQ), f32)
        for p in range(IDX_HEADS // 2):
            w = iq_ref[:, p * LANES:(p + 1) * LANES]
            ze = lax.dot_general(lo, w, _NT, preferred_element_type=f32)
            zo = lax.dot_general(hi, w, _NT, preferred_element_type=f32)
            acc = acc + iwt_ref[2 * p:2 * p + 1, :] * jnp.maximum(ze, 0.0)
            acc = acc + iwt_ref[2 * p + 1:2 * p + 2, :] * jnp.maximum(zo, 0.0)
        return acc

    def fold8(x, op):
        return op(x.reshape(x.shape[0] // SUBLANES, SUBLANES, x.shape[1]), axis=0)

    @pl.when(kt == 0)
    def _index_and_select():
        def full_block(c, carry):
            mx, mn = carry
            for half in range(TQ // TKI):
                r0 = pl.multiple_of(c * TQ + half * TKI, TKI)
                sc = score_chunk(r0)
                sc_ref[pl.ds(r0, TKI), :] = sc
                mx = jnp.maximum(mx, fold8(sc, jnp.max))
                mn = jnp.minimum(mn, fold8(sc, jnp.min))
            return mx, mn

        mx0 = jnp.full((SUBLANES, TQ), -jnp.inf, f32)
        mn0 = jnp.full((SUBLANES, TQ), jnp.inf, f32)
        mx, mn = lax.fori_loop(0, qi, full_block, (mx0, mn0))
        for dchunk in range(TQ // TKI):
            r0 = pl.multiple_of(qi * TQ + dchunk * TKI, TKI)
            sc = score_chunk(r0)
            row = lax.broadcasted_iota(jnp.int32, (TKI, TQ), 0) + dchunk * TKI
            col = lax.broadcasted_iota(jnp.int32, (TKI, TQ), 1)
            valid = row <= col
            sc_ref[pl.ds(r0, TKI), :] = jnp.where(valid, sc, -jnp.inf)
            mx = jnp.maximum(mx, fold8(jnp.where(valid, sc, -jnp.inf), jnp.max))
            mn = jnp.minimum(mn, fold8(jnp.where(valid, sc, jnp.inf), jnp.min))

        def fill_past(c, carry):
            sc_ref[pl.ds(pl.multiple_of(n_keys + c * TQ, TQ), TQ), :] = jnp.full((TQ, TQ), -jnp.inf, f32)
            return carry

        lax.fori_loop(0, (n_kt * TK - n_keys) // TQ, fill_past, 0)

        hi0 = jnp.max(mx, axis=0, keepdims=True)
        lo0 = jnp.min(mn, axis=0, keepdims=True)
        lane = lax.broadcasted_iota(jnp.int32, (1, TQ), 1)
        take_all = (qi * TQ + lane + 1) <= topk
        kf = f32(topk)
        n_cnt = n_keys // CNT_ROWS

        def count_ge(mid):
            def body(c, cnt):
                r0 = pl.multiple_of(c * CNT_ROWS, CNT_ROWS)
                ind = jnp.where(sc_ref[pl.ds(r0, CNT_ROWS), :] >= mid, f32(1), f32(0))
                return cnt + jnp.sum(ind.reshape(CNT_ROWS // CNT_ACC, CNT_ACC, TQ), axis=0)
            cnt = lax.fori_loop(0, n_cnt, body, jnp.zeros((CNT_ACC, TQ), f32))
            return jnp.sum(cnt, axis=0, keepdims=True)

        def cond(st):
            return jnp.logical_and(st[0] < MAX_BISECT, st[1] > 0)

        def body(st):
            it, _, lo, hi, thr, done = st
            mid = lo * 0.5 + hi * 0.5
            stuck = jnp.logical_or(mid <= lo, mid >= hi)
            cnt = count_ge(mid)
            exact = cnt == kf
            ge = cnt >= kf
            act = done < 0.5
            thr = jnp.where(jnp.logical_and(act, exact), mid,
                            jnp.where(jnp.logical_and(act, stuck), lo, thr))
            newly = jnp.logical_and(act, jnp.logical_or(exact, stuck))
            move = jnp.logical_and(act, jnp.logical_not(newly))
            lo = jnp.where(jnp.logical_and(move, ge), mid, lo)
            hi = jnp.where(jnp.logical_and(move, jnp.logical_not(ge)), mid, hi)
            done = jnp.where(newly, f32(1), done)
            n_act = jnp.sum(jnp.where(done < 0.5, jnp.int32(1), jnp.int32(0)))
            return it + 1, n_act, lo, hi, thr, done

        done0 = jnp.where(take_all, f32(1), f32(0))
        thr0 = jnp.full((1, TQ), F32_LOWEST, f32)
        n_act0 = jnp.sum(jnp.where(take_all, jnp.int32(0), jnp.int32(1)))
        st = lax.while_loop(cond, body, (jnp.int32(0), n_act0, lo0, hi0, thr0, done0))
        thr_ref[...] = st[4]

        m_ref[...] = jnp.full(m_ref.shape, MASK_NEG, f32)
        acc_ref[...] = jnp.zeros(acc_ref.shape, f32)

    k0 = pl.multiple_of(kt * TK, TK)
    bias_ref[...] = jnp.where(sc_ref[pl.ds(k0, TK), :] >= thr_ref[...], f32(0), f32(MASK_NEG))

    def logits(g):
        s = lax.dot_general(k_ref[g], q_ref[g], _NT, preferred_element_type=f32)
        return s + jnp.tile(bias_ref[...], (1, group))

    @pl.when(kt == 0)
    def _():
        redo_ref[...] = jnp.ones(redo_ref.shape, f32)
        flag_ref[0] = jnp.int32(1)

    @pl.when(kt > 0)
    def _fixed_base_tile():
        width = UNIT_HEADS * TQ
        per_group = group // UNIT_HEADS
        n_units = N_KV_HEADS * per_group

        def unit_logits(u):
            g, c0 = u // per_group, (u % per_group) * width
            s = lax.dot_general(k_ref[g], q_ref[g, c0:c0 + width, :], _NT, preferred_element_type=f32)
            return s + jnp.tile(bias_ref[...], (1, UNIT_HEADS))

        bad = jnp.zeros((1, width), f32)
        s_next = unit_logits(0)
        for u in range(n_units):
            g, c0 = u // per_group, (u % per_group) * width
            s_cur = s_next
            if u + 1 < n_units:
                s_next = unit_logits(u + 1)
            p = jnp.exp2(s_cur - m_ref[g, :, c0:c0 + width]).astype(jnp.bfloat16)
            pv = jnp.dot(vt_ref[g], p, preferred_element_type=f32)
            acc = acc_ref[g, :, c0:c0 + width]
            ok = (acc[HEAD_DIM:HEAD_DIM + 1, :] + pv[HEAD_DIM:HEAD_DIM + 1, :]) < DEN_LIMIT
            acc_ref[g, :, c0:c0 + width] = jnp.where(ok, acc + pv, acc)
            redo = jnp.where(ok, f32(0), f32(1))
            redo_ref[g, :, c0:c0 + width] = redo
            bad = jnp.maximum(bad, redo)
        flag_ref[0] = (jnp.max(bad) > 0.0).astype(jnp.int32)

    @pl.when(flag_ref[0] == 1)
    def _online_max_tile():
        def softmax_pv(g, s):
            redo = redo_ref[g] > 0.5
            m_prev = m_ref[g]
            m_new = jnp.where(redo, jnp.maximum(m_prev, jnp.max(s, axis=0, keepdims=True)), m_prev)
            alpha = jnp.exp2(m_prev - m_new)
            p = jnp.exp2(s - m_new).astype(jnp.bfloat16)
            pv = jnp.dot(vt_ref[g], p, preferred_element_type=f32)
            acc = acc_ref[g]
            acc_ref[g] = jnp.where(redo, alpha * acc + pv, acc)
            m_ref[g] = m_new

        s_next = logits(0)
        for g in range(N_KV_HEADS):
            s_cur = s_next
            if g + 1 < N_KV_HEADS:
                s_next = logits(g + 1)
            softmax_pv(g, s_cur)

    @pl.when(kt == n_kt - 1)
    def _finalize():
        for g in range(N_KV_HEADS):
            a = acc_ref[g]
            o = a[:HEAD_DIM] * (1.0 / a[HEAD_DIM:HEAD_DIM + 1])
            for j in range(group):
                h = g * group + j
                o_ref[:, h * HEAD_DIM:(h + 1) * HEAD_DIM] = o[:, j * TQ:(j + 1) * TQ].T.astype(o_ref.dtype)


def _dsa_steps(b, s):
    bi, qi, kt = [], [], []
    for bb in range(b):
        for q in range(s // TQ):
            for k in range(-(-((q + 1) * TQ) // TK)):
                bi.append(bb)
                qi.append(q)
                kt.append(k)
    return (np.asarray(bi, np.int32), np.asarray(qi, np.int32), np.asarray(kt, np.int32))


def _dsa(q, iq, iwt, iklo, ikhi, k, vt):
    b, _, nq, gtq, _ = q.shape
    group = gtq // TQ
    n_heads = N_KV_HEADS * group
    s = nq * TQ
    assert TK % TQ == 0 and TQ % TKI == 0 and TQ % CNT_ROWS == 0 and s % TK == 0
    topk = min(TOPK_MAX, s // 4)
    bi, qi, kt = _dsa_steps(b, s)
    iqw = iq.shape[-1]
    kern = functools.partial(_dsa_kernel, n_heads=n_heads, topk=topk)
    grid_spec = pltpu.PrefetchScalarGridSpec(
        num_scalar_prefetch=3,
        grid=(len(bi),),
        in_specs=[
            pl.BlockSpec((None, N_KV_HEADS, None, gtq, HEAD_DIM),
                         lambda t, b_, q_, k_: (b_[t], 0, q_[t], 0, 0)),
            pl.BlockSpec((None, TQ, iqw), lambda t, b_, q_, k_: (b_[t], q_[t], 0)),
            pl.BlockSpec((None, IDX_HEADS, TQ), lambda t, b_, q_, k_: (b_[t], 0, q_[t])),
            pl.BlockSpec((None, s, LANES), lambda t, b_, q_, k_: (b_[t], 0, 0)),
            pl.BlockSpec((None, s, LANES), lambda t, b_, q_, k_: (b_[t], 0, 0)),
            pl.BlockSpec((None, N_KV_HEADS, TK, HEAD_DIM), lambda t, b_, q_, k_: (b_[t], 0, k_[t], 0)),
            pl.BlockSpec((None, N_KV_HEADS, VD, TK), lambda t, b_, q_, k_: (b_[t], 0, 0, k_[t])),
        ],
        out_specs=pl.BlockSpec((None, TQ, n_heads * HEAD_DIM), lambda t, b_, q_, k_: (b_[t], q_[t], 0)),
        scratch_shapes=[
            pltpu.VMEM((s, TQ), jnp.float32),
            pltpu.VMEM((TK, TQ), jnp.float32),
            pltpu.VMEM((1, TQ), jnp.float32),
            pltpu.VMEM((N_KV_HEADS, 1, gtq), jnp.float32),
            pltpu.VMEM((N_KV_HEADS, VD, gtq), jnp.float32),
            pltpu.VMEM((N_KV_HEADS, 1, gtq), jnp.float32),
            pltpu.SMEM((1,), jnp.int32),
        ],
    )
    return pl.pallas_call(
        kern,
        grid_spec=grid_spec,
        out_shape=jax.ShapeDtypeStruct((b, s, n_heads * HEAD_DIM), jnp.bfloat16),
        compiler_params=_cparams(("arbitrary",)),
        name="dsa_attention",
    )(jnp.asarray(bi), jnp.asarray(qi), jnp.asarray(kt), q, iq, iwt, iklo, ikhi, k, vt)


CONV_ROWS = 64
CONV_COLS = 256


def _conv_kernel(ua_ref, ug_ref, ha_ref, hg_ref, w_ref, b_ref, g_ref, beta_ref, o_ref,
                 glu_scr, y_scr, h_scr, *, ts):
    i = pl.program_id(1)
    f32 = jnp.float32
    ua = ua_ref[...].astype(f32)
    ug = ug_ref[...].astype(f32)
    glu_scr[HALO:HALO + ts, :] = ua * jax.nn.sigmoid(ug)
    ha = ha_ref[...].astype(f32)
    hg = hg_ref[...].astype(f32)
    halo = ha * jax.nn.sigmoid(hg)
    glu_scr[0:HALO, :] = jnp.where(i > 0, halo, 0.0)

    c = y_scr.shape[1]
    off = HALO - (CONV_WIDTH - 1)
    for cs in range(0, c, CONV_COLS):
        cols = slice(cs, cs + CONV_COLS)
        wcs = w_ref[:, cols]
        for r0 in range(0, ts, CONV_ROWS):
            acc = jnp.zeros((CONV_ROWS, CONV_COLS), f32) + b_ref[:, cols]
            for sh in range(SUBLANES):
                rows = CONV_ROWS + (SUBLANES if sh else 0)
                part = None
                for j in range(CONV_WIDTH):
                    if (off + j) % SUBLANES != sh:
                        continue
                    a0 = r0 + (off + j) // SUBLANES * SUBLANES
                    term = wcs[j:j + 1, :] * glu_scr[a0:a0 + rows, cols]
                    part = term if part is None else part + term
                if sh == 0:
                    acc = acc + part
                else:
                    h_scr[sh, :, :] = part
                    acc = acc + h_scr[sh, sh:sh + CONV_ROWS, :]
            y_scr[r0:r0 + CONV_ROWS, cols] = acc

    y = y_scr[...]
    mu = jnp.mean(y, axis=-1, keepdims=True)
    yc = y - mu
    var = jnp.mean(yc * yc, axis=-1, keepdims=True)
    z = yc * lax.rsqrt(var + EPS) * g_ref[...] + beta_ref[...]
    o_ref[...] = (z * jax.nn.sigmoid(z)).astype(o_ref.dtype)


def _conv_module(proj, ua_col, ug_col, w, bias, g, beta, ts):
    b, s, _ = proj.shape
    c = w.shape[1]
    hb = ts // HALO
    kern = functools.partial(_conv_kernel, ts=ts)
    return pl.pallas_call(
        kern,
        grid=(b, s // ts),
        in_specs=[pl.BlockSpec((None, ts, c), lambda bi, i: (bi, i, ua_col // c)),
                  pl.BlockSpec((None, ts, c), lambda bi, i: (bi, i, ug_col // c)),
                  pl.BlockSpec((None, HALO, c), lambda bi, i: (bi, jnp.maximum(i * hb - 1, 0), ua_col // c)),
                  pl.BlockSpec((None, HALO, c), lambda bi, i: (bi, jnp.maximum(i * hb - 1, 0), ug_col // c)),
                  pl.BlockSpec((HALO, c), lambda bi, i: (0, 0)),
                  pl.BlockSpec((1, c), lambda bi, i: (0, 0)),
                  pl.BlockSpec((1, c), lambda bi, i: (0, 0)),
                  pl.BlockSpec((1, c), lambda bi, i: (0, 0))],
        out_specs=pl.BlockSpec((None, ts, c), lambda bi, i: (bi, i, 0)),
        out_shape=jax.ShapeDtypeStruct((b, s, c), jnp.bfloat16),
        scratch_shapes=[pltpu.VMEM((HALO + ts, c), jnp.float32),
                        pltpu.VMEM((ts, c), jnp.float32),
                        pltpu.VMEM((SUBLANES, CONV_ROWS + SUBLANES, CONV_COLS), jnp.float32)],
        compiler_params=_cparams(("parallel", "arbitrary")),
        name="conv_module",
    )(proj, proj, proj, proj, w, bias, g, beta)


def _rope_tables(pos, rot_dim, period):
    half = rot_dim // 2
    freqs = ROPE_THETA ** (-jnp.arange(half, dtype=jnp.float32) / half)
    ang = pos.astype(jnp.float32)[..., None] * freqs
    cos, sin = jnp.cos(ang), jnp.sin(ang)
    pad = period - rot_dim
    z = jnp.zeros(ang.shape[:2] + (pad,), jnp.float32)
    zh = jnp.zeros_like(sin)
    c = jnp.concatenate([cos, cos, z + 1.0], axis=-1)
    s1 = jnp.concatenate([-sin, zh, z], axis=-1)
    s2 = jnp.concatenate([zh, sin, z], axis=-1)
    tab = jnp.stack([c, s1, s2], axis=1)
    return jnp.tile(tab, (1, 1, 1, LANES // period))


def _pad_lanes(v, width):
    return jnp.pad(v, (0, width - v.shape[0]))[None, :].astype(jnp.float32)


def kernel(x, positions, norm_mix_g, w_in, idx_k_ln_g, idx_k_ln_b, conv_dw_w, conv_dw_b, conv_ln_g, conv_ln_b, w_conv_out, b_conv_out, w_out, norm_ffn_g, w_ffn_gate, w_ffn_up, w_ffn_down, final_norm_g):
    b, s, d = x.shape
    depth = w_in.shape[0]
    n_heads = d // HEAD_DIM
    kvw = N_KV_HEADS * HEAD_DIM
    iqw = IDX_HEADS * IDX_DIM
    conv_ch = conv_dw_w.shape[-1]
    m = b * s
    bf = jnp.bfloat16

    c_q1 = d
    c_k1 = c_q1 + kvw
    c_v1 = c_k1 + kvw
    c_iq1 = c_v1 + iqw
    c_ik1 = c_iq1 + IDX_DIM
    c_iw1 = c_ik1 + IDX_HEADS
    c_glu1 = c_iw1 + 2 * conv_ch
    c_gate1 = c_glu1 + 2 * d
    ua_col, ug_col = d, d + conv_ch
    ga_col = d + 2 * conv_ch
    gc_col = ga_col + d
    k_col = gc_col + d
    v_col = k_col + kvw

    tab_h = _rope_tables(positions, ROPE_DIM, HEAD_DIM)
    tab_i = _rope_tables(positions, IDX_ROPE_DIM, IDX_DIM)

    tm = min(1024, m)
    xf = x.reshape(m, d)
    for l in range(depth):
        wl = w_in[l]
        w_slab = jnp.concatenate(
            [wl[:, :c_q1], wl[:, c_iw1:c_glu1], wl[:, c_glu1:c_gate1], wl[:, c_q1:c_v1]],
            axis=1).astype(bf)
        w_idx = jnp.pad(wl[:, c_v1:c_iw1], ((0, 0), (0, LANES - IDX_DIM - IDX_HEADS))).astype(bf)
        g_mix = norm_mix_g[l][None, :]

        proj = _norm_proj(xf, g_mix, w_slab, bf, tm, 512).reshape(b, s, -1)
        fproj = _norm_proj(xf, g_mix, w_idx, jnp.float32, tm, w_idx.shape[1]).reshape(b, s, -1)

        q, k, vt, iq, iklo, ikhi, iwp = _prep(
            proj, fproj, tab_h, tab_i,
            _pad_lanes(idx_k_ln_g[l], LANES), _pad_lanes(idx_k_ln_b[l], LANES),
            n_heads, k_col, v_col, TQ)
        iwt = jnp.swapaxes(iwp[:, :, IDX_DIM:IDX_DIM + IDX_HEADS], 1, 2)
        attn = _dsa(q, iq, iwt, iklo, ikhi, k, vt)

        dw = jnp.pad(conv_dw_w[l], ((0, HALO - CONV_WIDTH), (0, 0)))
        u = _conv_module(proj, ua_col, ug_col, dw, conv_dw_b[l][None, :],
                         conv_ln_g[l][None, :], conv_ln_b[l][None, :], TQ)

        y = _mix(u.reshape(m, conv_ch), w_conv_out[l].astype(bf), b_conv_out[l][None, :],
                 proj.reshape(m, -1), ga_col, gc_col, attn.reshape(m, d), tm, 512)
        xf = _mm_res(y, w_out[l].astype(bf), xf, tm, 512)

        t = _ffn_up(xf, norm_ffn_g[l][None, :], w_ffn_gate[l].astype(bf), w_ffn_up[l].astype(bf), tm, 512)
        xf = _mm_res(t, w_ffn_down[l].astype(bf), xf, tm, 512)

    out = _final_norm(xf, final_norm_g[None, :], tm)
    return out.reshape(b, s, d)
```

```python
import functools
import math

import numpy as np
import jax
import jax.numpy as jnp
from jax import lax
from jax.experimental import pallas as pl
from jax.experimental.pallas import tpu as pltpu

HEAD_DIM = 128
N_KV_HEADS = 4
ROPE_DIM = HEAD_DIM // 4
ROPE_THETA = 500000.0
IDX_HEADS = 16
IDX_DIM = 64
IDX_ROPE_DIM = IDX_DIM // 4
TOPK_MAX = 256
CONV_WIDTH = 31
EPS = 1e-6

LANES = 128
SUBLANES = 8
BF16_ROWS = 16
VD = HEAD_DIM + BF16_ROWS
VMEM_LIMIT = 56 * 1024 * 1024

TQ = 256
TK = 1024
TKI = 128
P1_BLOCKS = 4
UNIT_HEADS = 2
CNT_ROWS = 256
CNT_ACC = 32
HALO = 32
MASK_NEG = -1e30
DEN_LIMIT = 2.0 ** 64
F32_LOWEST = float(np.finfo(np.float32).min)
MAX_BISECT = 200

_NT = (((1,), (1,)), ((), ()))


def _cparams(sem):
    return pltpu.CompilerParams(dimension_semantics=sem, vmem_limit_bytes=VMEM_LIMIT)


def _rms_to_bf16(x, g):
    ms = jnp.mean(x * x, axis=-1, keepdims=True)
    return (x * lax.rsqrt(ms + EPS) * g).astype(jnp.bfloat16)


def _norm_proj_kernel(x_ref, g_ref, w_ref, o_ref, h_scr):
    @pl.when(pl.program_id(1) == 0)
    def _():
        h_scr[...] = _rms_to_bf16(x_ref[...], g_ref[...])

    o_ref[...] = jnp.dot(h_scr[...], w_ref[...],
                         preferred_element_type=jnp.float32).astype(o_ref.dtype)


def _norm_proj(x, g, w, out_dtype, tm, tn):
    m, d = x.shape
    n = w.shape[1]
    return pl.pallas_call(
        _norm_proj_kernel,
        grid=(m // tm, n // tn),
        in_specs=[pl.BlockSpec((tm, d), lambda i, j: (i, 0)),
                  pl.BlockSpec((1, d), lambda i, j: (0, 0)),
                  pl.BlockSpec((d, tn), lambda i, j: (0, j))],
        out_specs=pl.BlockSpec((tm, tn), lambda i, j: (i, j)),
        out_shape=jax.ShapeDtypeStruct((m, n), out_dtype),
        scratch_shapes=[pltpu.VMEM((tm, d), jnp.bfloat16)],
        compiler_params=_cparams(("parallel", "arbitrary")),
        name="norm_proj",
    )(x, g, w)


def _ffn_up_kernel(x_ref, g_ref, wg_ref, wu_ref, o_ref, h_scr):
    @pl.when(pl.program_id(1) == 0)
    def _():
        h_scr[...] = _rms_to_bf16(x_ref[...], g_ref[...])

    h = h_scr[...]
    a = jnp.dot(h, wg_ref[...], preferred_element_type=jnp.float32)
    b = jnp.dot(h, wu_ref[...], preferred_element_type=jnp.float32)
    o_ref[...] = (a * jax.nn.sigmoid(a) * b).astype(o_ref.dtype)


def _ffn_up(x, g, wg, wu, tm, tn):
    m, d = x.shape
    n = wg.shape[1]
    return pl.pallas_call(
        _ffn_up_kernel,
        grid=(m // tm, n // tn),
        in_specs=[pl.BlockSpec((tm, d), lambda i, j: (i, 0)),
                  pl.BlockSpec((1, d), lambda i, j: (0, 0)),
                  pl.BlockSpec((d, tn), lambda i, j: (0, j)),
                  pl.BlockSpec((d, tn), lambda i, j: (0, j))],
        out_specs=pl.BlockSpec((tm, tn), lambda i, j: (i, j)),
        out_shape=jax.ShapeDtypeStruct((m, n), jnp.bfloat16),
        scratch_shapes=[pltpu.VMEM((tm, d), jnp.bfloat16)],
        compiler_params=_cparams(("parallel", "arbitrary")),
        name="ffn_up",
    )(x, g, wg, wu)


def _mm_res_kernel(a_ref, w_ref, r_ref, o_ref):
    o_ref[...] = r_ref[...] + jnp.dot(a_ref[...], w_ref[...],
                                      preferred_element_type=jnp.float32)


def _mm_res(a, w, res, tm, tn):
    m, k = a.shape
    n = w.shape[1]
    return pl.pallas_call(
        _mm_res_kernel,
        grid=(m // tm, n // tn),
        in_specs=[pl.BlockSpec((tm, k), lambda i, j: (i, 0)),
                  pl.BlockSpec((k, tn), lambda i, j: (0, j)),
                  pl.BlockSpec((tm, tn), lambda i, j: (i, j))],
        out_specs=pl.BlockSpec((tm, tn), lambda i, j: (i, j)),
        out_shape=jax.ShapeDtypeStruct((m, n), jnp.float32),
        compiler_params=_cparams(("parallel", "arbitrary")),
        name="mm_res",
    )(a, w, res)


def _mix_kernel(u_ref, w_ref, b_ref, ga_ref, gc_ref, at_ref, y_ref):
    conv = jnp.dot(u_ref[...], w_ref[...], preferred_element_type=jnp.float32) + b_ref[...]
    ga = jax.nn.sigmoid(ga_ref[...].astype(jnp.float32))
    gc = jax.nn.sigmoid(gc_ref[...].astype(jnp.float32))
    y_ref[...] = (ga * at_ref[...].astype(jnp.float32) + gc * conv).astype(y_ref.dtype)


def _mix(u, w, b, proj, ga_col, gc_col, attn, tm, tn):
    m, k = u.shape
    n = w.shape[1]
    ga_blk = ga_col // tn
    gc_blk = gc_col // tn
    return pl.pallas_call(
        _mix_kernel,
        grid=(m // tm, n // tn),
        in_specs=[pl.BlockSpec((tm, k), lambda i, j: (i, 0)),
                  pl.BlockSpec((k, tn), lambda i, j: (0, j)),
                  pl.BlockSpec((1, tn), lambda i, j: (0, j)),
                  pl.BlockSpec((tm, tn), lambda i, j: (i, ga_blk + j)),
                  pl.BlockSpec((tm, tn), lambda i, j: (i, gc_blk + j)),
                  pl.BlockSpec((tm, tn), lambda i, j: (i, j))],
        out_specs=pl.BlockSpec((tm, tn), lambda i, j: (i, j)),
        out_shape=jax.ShapeDtypeStruct((m, n), jnp.bfloat16),
        compiler_params=_cparams(("parallel", "arbitrary")),
        name="gated_mix",
    )(u, w, b, proj, proj, attn)


def _final_norm_kernel(x_ref, g_ref, o_ref):
    x = x_ref[...]
    ms = jnp.mean(x * x, axis=-1, keepdims=True)
    o_ref[...] = x * lax.rsqrt(ms + EPS) * g_ref[...]


def _final_norm(x, g, tm):
    m, d = x.shape
    return pl.pallas_call(
        _final_norm_kernel,
        grid=(m // tm,),
        in_specs=[pl.BlockSpec((tm, d), lambda i: (i, 0)),
                  pl.BlockSpec((1, d), lambda i: (0, 0))],
        out_specs=pl.BlockSpec((tm, d), lambda i: (i, 0)),
        out_shape=jax.ShapeDtypeStruct((m, d), jnp.float32),
        compiler_params=_cparams(("parallel",)),
        name="final_norm",
    )(x, g)


def _rope(x, c, s1, s2, half):
    return (x * c + pltpu.roll(x, LANES - half, axis=1) * s1
            + pltpu.roll(x, half, axis=1) * s2)


def _prep_kernel(q_ref, k_ref, v_ref, f_ref, th_ref, ti_ref, lng_ref, lnb_ref,
                 qo_ref, ko_ref, vto_ref, iqo_ref, iklo_ref, ikhi_ref, iwo_ref,
                 *, n_heads, q_scale, iw_scale):
    ch, s1h, s2h = th_ref[0], th_ref[1], th_ref[2]
    ci, s1i, s2i = ti_ref[0], ti_ref[1], ti_ref[2]
    hh = ROPE_DIM // 2
    hi = IDX_ROPE_DIM // 2
    group = n_heads // N_KV_HEADS
    tm = q_ref.shape[0]
    for h in range(n_heads):
        xh = q_ref[:, h * HEAD_DIM:(h + 1) * HEAD_DIM].astype(jnp.float32)
        j = h % group
        qo_ref[h // group, j * tm:(j + 1) * tm, :] = (
            _rope(xh, ch, s1h, s2h, hh) * q_scale).astype(qo_ref.dtype)
    for g in range(N_KV_HEADS):
        xk = k_ref[:, g * HEAD_DIM:(g + 1) * HEAD_DIM].astype(jnp.float32)
        ko_ref[g] = _rope(xk, ch, s1h, s2h, hh).astype(ko_ref.dtype)
        xv = v_ref[:, g * HEAD_DIM:(g + 1) * HEAD_DIM].astype(jnp.float32)
        vto_ref[g, 0:HEAD_DIM, :] = xv.T.astype(vto_ref.dtype)
        vto_ref[g, HEAD_DIM:VD, :] = jnp.ones((BF16_ROWS, tm), vto_ref.dtype)
    n_pairs = IDX_HEADS * IDX_DIM // LANES
    for p in range(n_pairs):
        xi = f_ref[:, p * LANES:(p + 1) * LANES]
        iqo_ref[:, p * LANES:(p + 1) * LANES] = _rope(xi, ci, s1i, s2i, hi).astype(iqo_ref.dtype)
    xt = f_ref[:, n_pairs * LANES:(n_pairs + 1) * LANES]
    lane = lax.broadcasted_iota(jnp.int32, xt.shape, 1)
    is_ik = lane < IDX_DIM
    mu = jnp.sum(jnp.where(is_ik, xt, 0.0), axis=-1, keepdims=True) * (1.0 / IDX_DIM)
    xc = jnp.where(is_ik, xt - mu, 0.0)
    var = jnp.sum(xc * xc, axis=-1, keepdims=True) * (1.0 / IDX_DIM)
    y = xc * lax.rsqrt(var + EPS) * lng_ref[...] + lnb_ref[...]
    r = _rope(y, ci, s1i, s2i, hi)
    iklo_ref[...] = r.astype(iklo_ref.dtype)
    ikhi_ref[...] = pltpu.roll(r, IDX_DIM, axis=1).astype(ikhi_ref.dtype)
    iwo_ref[...] = xt * iw_scale


def _prep(proj, fproj, tab_h, tab_i, lng, lnb, n_heads, k_col, v_col, tm):
    b, s, _ = proj.shape
    d = n_heads * HEAD_DIM
    group = n_heads // N_KV_HEADS
    kvw = N_KV_HEADS * HEAD_DIM
    fw = fproj.shape[-1]
    iqw = IDX_HEADS * IDX_DIM
    kern = functools.partial(
        _prep_kernel, n_heads=n_heads,
        q_scale=(HEAD_DIM ** -0.5) * math.log2(math.e),
        iw_scale=(IDX_HEADS ** -0.5) * (IDX_DIM ** -0.5))
    bf = jnp.bfloat16
    return pl.pallas_call(
        kern,
        grid=(b, s // tm),
        in_specs=[pl.BlockSpec((None, tm, d), lambda bi, i: (bi, i, 0)),
                  pl.BlockSpec((None, tm, kvw), lambda bi, i: (bi, i, k_col // kvw)),
                  pl.BlockSpec((None, tm, kvw), lambda bi, i: (bi, i, v_col // kvw)),
                  pl.BlockSpec((None, tm, fw), lambda bi, i: (bi, i, 0)),
                  pl.BlockSpec((None, 3, tm, LANES), lambda bi, i: (bi, 0, i, 0)),
                  pl.BlockSpec((None, 3, tm, LANES), lambda bi, i: (bi, 0, i, 0)),
                  pl.BlockSpec((1, LANES), lambda bi, i: (0, 0)),
                  pl.BlockSpec((1, LANES), lambda bi, i: (0, 0))],
        out_specs=[pl.BlockSpec((None, N_KV_HEADS, None, group * tm, HEAD_DIM),
                                lambda bi, i: (bi, 0, i, 0, 0)),
                   pl.BlockSpec((None, N_KV_HEADS, tm, HEAD_DIM), lambda bi, i: (bi, 0, i, 0)),
                   pl.BlockSpec((None, N_KV_HEADS, VD, tm), lambda bi, i: (bi, 0, 0, i)),
                   pl.BlockSpec((None, tm, iqw), lambda bi, i: (bi, i, 0)),
                   pl.BlockSpec((None, tm, LANES), lambda bi, i: (bi, i, 0)),
                   pl.BlockSpec((None, tm, LANES), lambda bi, i: (bi, i, 0)),
                   pl.BlockSpec((None, tm, LANES), lambda bi, i: (bi, i, 0))],
        out_shape=[jax.ShapeDtypeStruct((b, N_KV_HEADS, s // tm, group * tm, HEAD_DIM), bf),
                   jax.ShapeDtypeStruct((b, N_KV_HEADS, s, HEAD_DIM), bf),
                   jax.ShapeDtypeStruct((b, N_KV_HEADS, VD, s), bf),
                   jax.ShapeDtypeStruct((b, s, iqw), bf),
                   jax.ShapeDtypeStruct((b, s, LANES), bf),
                   jax.ShapeDtypeStruct((b, s, LANES), bf),
                   jax.ShapeDtypeStruct((b, s, LANES), jnp.float32)],
        compiler_params=_cparams(("parallel", "parallel")),
        name="rope_prep",
    )(proj, proj, proj, fproj, tab_h, tab_i, lng, lnb)


def _dsa_kernel(bi_ref, qi_ref, kt_ref,
                q_ref, iq_ref, iwt_ref, iklo_ref, ikhi_ref, k_ref, vt_ref,
                o_ref,
                sc_ref, stat_ref, bias_ref, thr_ref, m_ref, acc_ref, redo_ref, flag_ref,
                *, n_heads, topk):
    t = pl.program_id(0)
    qi = qi_ref[t]
    kt = kt_ref[t]
    n_keys = (qi + 1) * TQ
    n_kt = (n_keys + TK - 1) // TK
    group = n_heads // N_KV_HEADS
    f32 = jnp.float32

    def score_chunk(r0):
        lo = iklo_ref[pl.ds(r0, TKI), :]
        hi = ikhi_ref[pl.ds(r0, TKI), :]
        acc = jnp.zeros((TKI, TQ), f32)
        for p in range(IDX_HEADS // 2):
            w = iq_ref[:, p * LANES:(p + 1) * LANES]
            ze = lax.dot_general(lo, w, _NT, preferred_element_type=f32)
            zo = lax.dot_general(hi, w, _NT, preferred_element_type=f32)
            acc = acc + iwt_ref[2 * p:2 * p + 1, :] * jnp.maximum(ze, 0.0)
            acc = acc + iwt_ref[2 * p + 1:2 * p + 2, :] * jnp.maximum(zo, 0.0)
        return acc

    def fold8(x, op):
        return op(x.reshape(x.shape[0] // SUBLANES, SUBLANES, x.shape[1]), axis=0)

    @pl.when(kt == 0)
    def _index_and_select():
        chunks_per_block = TQ // TKI

        def score_blocks(first_block, n_blocks):
            mx, mn = stat_ref[0], stat_ref[1]
            for ch in range(n_blocks * chunks_per_block):
                r0 = pl.multiple_of(first_block * TQ + ch * TKI, TKI)
                sc = score_chunk(r0)
                sc_ref[pl.ds(r0, TKI), :] = sc
                mx = jnp.maximum(mx, fold8(sc, jnp.max))
                mn = jnp.minimum(mn, fold8(sc, jnp.min))
            stat_ref[0], stat_ref[1] = mx, mn

        stat_ref[0] = jnp.full((SUBLANES, TQ), -jnp.inf, f32)
        stat_ref[1] = jnp.full((SUBLANES, TQ), jnp.inf, f32)

        def many_blocks(c, carry):
            score_blocks(c * P1_BLOCKS, P1_BLOCKS)
            return carry

        lax.fori_loop(0, qi // P1_BLOCKS, many_blocks, 0)
        n_blocks = P1_BLOCKS // 2
        while n_blocks >= 1:
            @pl.when((qi & n_blocks) != 0)
            def _(n_blocks=n_blocks):
                score_blocks(qi // (2 * n_blocks) * (2 * n_blocks), n_blocks)
            n_blocks //= 2

        mx, mn = stat_ref[0], stat_ref[1]
        for dchunk in range(chunks_per_block):
            r0 = pl.multiple_of(qi * TQ + dchunk * TKI, TKI)
            sc = score_chunk(r0)
            row = lax.broadcasted_iota(jnp.int32, (TKI, TQ), 0) + dchunk * TKI
            col = lax.broadcasted_iota(jnp.int32, (TKI, TQ), 1)
            valid = row <= col
            sc_ref[pl.ds(r0, TKI), :] = jnp.where(valid, sc, -jnp.inf)
            mx = jnp.maximum(mx, fold8(jnp.where(valid, sc, -jnp.inf), jnp.max))
            mn = jnp.minimum(mn, fold8(jnp.where(valid, sc, jnp.inf), jnp.min))

        def fill_past(c, carry):
            sc_ref[pl.ds(pl.multiple_of(n_keys + c * TQ, TQ), TQ), :] = jnp.full((TQ, TQ), -jnp.inf, f32)
            return carry

        lax.fori_loop(0, (n_kt * TK - n_keys) // TQ, fill_past, 0)

        hi0 = jnp.max(mx, axis=0, keepdims=True)
        lo0 = jnp.min(mn, axis=0, keepdims=True)
        lane = lax.broadcasted_iota(jnp.int32, (1, TQ), 1)
        take_all = (qi * TQ + lane + 1) <= topk
        kf = f32(topk)
        n_cnt = n_keys // CNT_ROWS

        def count_ge(mid):
            def body(c, cnt):
                r0 = pl.multiple_of(c * CNT_ROWS, CNT_ROWS)
                ind = jnp.where(sc_ref[pl.ds(r0, CNT_ROWS), :] >= mid, f32(1), f32(0))
                return cnt + jnp.sum(ind.reshape(CNT_ROWS // CNT_ACC, CNT_ACC, TQ), axis=0)
            cnt = lax.fori_loop(0, n_cnt, body, jnp.zeros((CNT_ACC, TQ), f32))
            return jnp.sum(cnt, axis=0, keepdims=True)

        def cond(st):
            return jnp.logical_and(st[0] < MAX_BISECT, st[1] > 0)

        def body(st):
            it, _, lo, hi, thr, done = st
            mid = lo * 0.5 + hi * 0.5
            stuck = jnp.logical_or(mid <= lo, mid >= hi)
            cnt = count_ge(mid)
            exact = cnt == kf
            ge = cnt >= kf
            act = done < 0.5
            thr = jnp.where(jnp.logical_and(act, exact), mid,
                            jnp.where(jnp.logical_and(act, stuck), lo, thr))
            newly = jnp.logical_and(act, jnp.logical_or(exact, stuck))
            move = jnp.logical_and(act, jnp.logical_not(newly))
            lo = jnp.where(jnp.logical_and(move, ge), mid, lo)
            hi = jnp.where(jnp.logical_and(move, jnp.logical_not(ge)), mid, hi)
            done = jnp.where(newly, f32(1), done)
            n_act = jnp.sum(jnp.where(done < 0.5, jnp.int32(1), jnp.int32(0)))
            return it + 1, n_act, lo, hi, thr, done

        done0 = jnp.where(take_all, f32(1), f32(0))
        thr0 = jnp.full((1, TQ), F32_LOWEST, f32)
        n_act0 = jnp.sum(jnp.where(take_all, jnp.int32(0), jnp.int32(1)))
        st = lax.while_loop(cond, body, (jnp.int32(0), n_act0, lo0, hi0, thr0, done0))
        thr_ref[...] = st[4]

        m_ref[...] = jnp.full(m_ref.shape, MASK_NEG, f32)
        acc_ref[...] = jnp.zeros(acc_ref.shape, f32)

    k0 = pl.multiple_of(kt * TK, TK)
    bias_ref[...] = jnp.where(sc_ref[pl.ds(k0, TK), :] >= thr_ref[...], f32(0), f32(MASK_NEG))

    def logits(g):
        s = lax.dot_general(k_ref[g], q_ref[g], _NT, preferred_element_type=f32)
        return s + jnp.tile(bias_ref[...], (1, group))

    @pl.when(kt == 0)
    def _():
        redo_ref[...] = jnp.ones(redo_ref.shape, f32)
        flag_ref[0] = jnp.int32(1)

    @pl.when(kt > 0)
    def _fixed_base_tile():
        width = UNIT_HEADS * TQ
        per_group = group // UNIT_HEADS
        n_units = N_KV_HEADS * per_group

        def unit_logits(u):
            g, c0 = u // per_group, (u % per_group) * width
            s = lax.dot_general(k_ref[g], q_ref[g, c0:c0 + width, :], _NT, preferred_element_type=f32)
            return s + jnp.tile(bias_ref[...], (1, UNIT_HEADS))

        bad = jnp.zeros((1, width), f32)
        s_next = unit_logits(0)
        for u in range(n_units):
            g, c0 = u // per_group, (u % per_group) * width
            s_cur = s_next
            if u + 1 < n_units:
                s_next = unit_logits(u + 1)
            p = jnp.exp2(s_cur - m_ref[g, :, c0:c0 + width]).astype(jnp.bfloat16)
            pv = jnp.dot(vt_ref[g], p, preferred_element_type=f32)
            acc = acc_ref[g, :, c0:c0 + width]
            ok = (acc[HEAD_DIM:HEAD_DIM + 1, :] + pv[HEAD_DIM:HEAD_DIM + 1, :]) < DEN_LIMIT
            acc_ref[g, :, c0:c0 + width] = jnp.where(ok, acc + pv, acc)
            redo = jnp.where(ok, f32(0), f32(1))
            redo_ref[g, :, c0:c0 + width] = redo
            bad = jnp.maximum(bad, redo)
        flag_ref[0] = (jnp.max(bad) > 0.0).astype(jnp.int32)

    @pl.when(flag_ref[0] == 1)
    def _online_max_tile():
        def softmax_pv(g, s):
            redo = redo_ref[g] > 0.5
            m_prev = m_ref[g]
            m_new = jnp.where(redo, jnp.maximum(m_prev, jnp.max(s, axis=0, keepdims=True)), m_prev)
            alpha = jnp.exp2(m_prev - m_new)
            p = jnp.exp2(s - m_new).astype(jnp.bfloat16)
            pv = jnp.dot(vt_ref[g], p, preferred_element_type=f32)
            acc = acc_ref[g]
            acc_ref[g] = jnp.where(redo, alpha * acc + pv, acc)
            m_ref[g] = m_new

        s_next = logits(0)
        for g in range(N_KV_HEADS):
            s_cur = s_next
            if g + 1 < N_KV_HEADS:
                s_next = logits(g + 1)
            softmax_pv(g, s_cur)

    @pl.when(kt == n_kt - 1)
    def _finalize():
        for g in range(N_KV_HEADS):
            a = acc_ref[g]
            o = a[:HEAD_DIM] * (1.0 / a[HEAD_DIM:HEAD_DIM + 1])
            for j in range(group):
                h = g * group + j
                o_ref[:, h * HEAD_DIM:(h + 1) * HEAD_DIM] = o[:, j * TQ:(j + 1) * TQ].T.astype(o_ref.dtype)


def _dsa_steps(b, s):
    bi, qi, kt = [], [], []
    for bb in range(b):
        for q in range(s // TQ):
            for k in range(-(-((q + 1) * TQ) // TK)):
                bi.append(bb)
                qi.append(q)
                kt.append(k)
    return (np.asarray(bi, np.int32), np.asarray(qi, np.int32), np.asarray(kt, np.int32))


def _dsa(q, iq, iwt, iklo, ikhi, k, vt):
    b, _, nq, gtq, _ = q.shape
    group = gtq // TQ
    n_heads = N_KV_HEADS * group
    s = nq * TQ
    assert TK % TQ == 0 and TQ % TKI == 0 and TQ % CNT_ROWS == 0 and s % TK == 0
    topk = min(TOPK_MAX, s // 4)
    bi, qi, kt = _dsa_steps(b, s)
    iqw = iq.shape[-1]
    kern = functools.partial(_dsa_kernel, n_heads=n_heads, topk=topk)
    grid_spec = pltpu.PrefetchScalarGridSpec(
        num_scalar_prefetch=3,
        grid=(len(bi),),
        in_specs=[
            pl.BlockSpec((None, N_KV_HEADS, None, gtq, HEAD_DIM),
                         lambda t, b_, q_, k_: (b_[t], 0, q_[t], 0, 0)),
            pl.BlockSpec((None, TQ, iqw), lambda t, b_, q_, k_: (b_[t], q_[t], 0)),
            pl.BlockSpec((None, IDX_HEADS, TQ), lambda t, b_, q_, k_: (b_[t], 0, q_[t])),
            pl.BlockSpec((None, s, LANES), lambda t, b_, q_, k_: (b_[t], 0, 0)),
            pl.BlockSpec((None, s, LANES), lambda t, b_, q_, k_: (b_[t], 0, 0)),
            pl.BlockSpec((None, N_KV_HEADS, TK, HEAD_DIM), lambda t, b_, q_, k_: (b_[t], 0, k_[t], 0)),
            pl.BlockSpec((None, N_KV_HEADS, VD, TK), lambda t, b_, q_, k_: (b_[t], 0, 0, k_[t])),
        ],
        out_specs=pl.BlockSpec((None, TQ, n_heads * HEAD_DIM), lambda t, b_, q_, k_: (b_[t], q_[t], 0)),
        scratch_shapes=[
            pltpu.VMEM((s, TQ), jnp.float32),
            pltpu.VMEM((2, SUBLANES, TQ), jnp.float32),
            pltpu.VMEM((TK, TQ), jnp.float32),
            pltpu.VMEM((1, TQ), jnp.float32),
            pltpu.VMEM((N_KV_HEADS, 1, gtq), jnp.float32),
            pltpu.VMEM((N_KV_HEADS, VD, gtq), jnp.float32),
            pltpu.VMEM((N_KV_HEADS, 1, gtq), jnp.float32),
            pltpu.SMEM((1,), jnp.int32),
        ],
    )
    return pl.pallas_call(
        kern,
        grid_spec=grid_spec,
        out_shape=jax.ShapeDtypeStruct((b, s, n_heads * HEAD_DIM), jnp.bfloat16),
        compiler_params=_cparams(("arbitrary",)),
        name="dsa_attention",
    )(jnp.asarray(bi), jnp.asarray(qi), jnp.asarray(kt), q, iq, iwt, iklo, ikhi, k, vt)


CONV_ROWS = 64
CONV_COLS = 256


def _conv_kernel(ua_ref, ug_ref, ha_ref, hg_ref, w_ref, b_ref, g_ref, beta_ref, o_ref,
                 glu_scr, y_scr, h_scr, *, ts):
    i = pl.program_id(1)
    f32 = jnp.float32
    ua = ua_ref[...].astype(f32)
    ug = ug_ref[...].astype(f32)
    glu_scr[HALO:HALO + ts, :] = ua * jax.nn.sigmoid(ug)
    ha = ha_ref[...].astype(f32)
    hg = hg_ref[...].astype(f32)
    halo = ha * jax.nn.sigmoid(hg)
    glu_scr[0:HALO, :] = jnp.where(i > 0, halo, 0.0)

    c = y_scr.shape[1]
    off = HALO - (CONV_WIDTH - 1)
    for cs in range(0, c, CONV_COLS):
        cols = slice(cs, cs + CONV_COLS)
        wcs = w_ref[:, cols]
        for r0 in range(0, ts, CONV_ROWS):
            acc = jnp.zeros((CONV_ROWS, CONV_COLS), f32) + b_ref[:, cols]
            for sh in range(SUBLANES):
                rows = CONV_ROWS + (SUBLANES if sh else 0)
                part = None
                for j in range(CONV_WIDTH):
                    if (off + j) % SUBLANES != sh:
                        continue
                    a0 = r0 + (off + j) // SUBLANES * SUBLANES
                    term = wcs[j:j + 1, :] * glu_scr[a0:a0 + rows, cols]
                    part = term if part is None else part + term
                if sh == 0:
                    acc = acc + part
                else:
                    h_scr[sh, :, :] = part
                    acc = acc + h_scr[sh, sh:sh + CONV_ROWS, :]
            y_scr[r0:r0 + CONV_ROWS, cols] = acc

    y = y_scr[...]
    mu = jnp.mean(y, axis=-1, keepdims=True)
    yc = y - mu
    var = jnp.mean(yc * yc, axis=-1, keepdims=True)
    z = yc * lax.rsqrt(var + EPS) * g_ref[...] + beta_ref[...]
    o_ref[...] = (z * jax.nn.sigmoid(z)).astype(o_ref.dtype)


def _conv_module(proj, ua_col, ug_col, w, bias, g, beta, ts):
    b, s, _ = proj.shape
    c = w.shape[1]
    hb = ts // HALO
    kern = functools.partial(_conv_kernel, ts=ts)
    return pl.pallas_call(
        kern,
        grid=(b, s // ts),
        in_specs=[pl.BlockSpec((None, ts, c), lambda bi, i: (bi, i, ua_col // c)),
                  pl.BlockSpec((None, ts, c), lambda bi, i: (bi, i, ug_col // c)),
                  pl.BlockSpec((None, HALO, c), lambda bi, i: (bi, jnp.maximum(i * hb - 1, 0), ua_col // c)),
                  pl.BlockSpec((None, HALO, c), lambda bi, i: (bi, jnp.maximum(i * hb - 1, 0), ug_col // c)),
                  pl.BlockSpec((HALO, c), lambda bi, i: (0, 0)),
                  pl.BlockSpec((1, c), lambda bi, i: (0, 0)),
                  pl.BlockSpec((1, c), lambda bi, i: (0, 0)),
                  pl.BlockSpec((1, c), lambda bi, i: (0, 0))],
        out_specs=pl.BlockSpec((None, ts, c), lambda bi, i: (bi, i, 0)),
        out_shape=jax.ShapeDtypeStruct((b, s, c), jnp.bfloat16),
        scratch_shapes=[pltpu.VMEM((HALO + ts, c), jnp.float32),
                        pltpu.VMEM((ts, c), jnp.float32),
                        pltpu.VMEM((SUBLANES, CONV_ROWS + SUBLANES, CONV_COLS), jnp.float32)],
        compiler_params=_cparams(("parallel", "arbitrary")),
        name="conv_module",
    )(proj, proj, proj, proj, w, bias, g, beta)


def _rope_tables(pos, rot_dim, period):
    half = rot_dim // 2
    freqs = ROPE_THETA ** (-jnp.arange(half, dtype=jnp.float32) / half)
    ang = pos.astype(jnp.float32)[..., None] * freqs
    cos, sin = jnp.cos(ang), jnp.sin(ang)
    pad = period - rot_dim
    z = jnp.zeros(ang.shape[:2] + (pad,), jnp.float32)
    zh = jnp.zeros_like(sin)
    c = jnp.concatenate([cos, cos, z + 1.0], axis=-1)
    s1 = jnp.concatenate([-sin, zh, z], axis=-1)
    s2 = jnp.concatenate([zh, sin, z], axis=-1)
    tab = jnp.stack([c, s1, s2], axis=1)
    return jnp.tile(tab, (1, 1, 1, LANES // period))


def _pad_lanes(v, width):
    return jnp.pad(v, (0, width - v.shape[0]))[None, :].astype(jnp.float32)


def kernel(x, positions, norm_mix_g, w_in, idx_k_ln_g, idx_k_ln_b, conv_dw_w, conv_dw_b, conv_ln_g, conv_ln_b, w_conv_out, b_conv_out, w_out, norm_ffn_g, w_ffn_gate, w_ffn_up, w_ffn_down, final_norm_g):
    b, s, d = x.shape
    depth = w_in.shape[0]
    n_heads = d // HEAD_DIM
    kvw = N_KV_HEADS * HEAD_DIM
    iqw = IDX_HEADS * IDX_DIM
    conv_ch = conv_dw_w.shape[-1]
    m = b * s
    bf = jnp.bfloat16

    c_q1 = d
    c_k1 = c_q1 + kvw
    c_v1 = c_k1 + kvw
    c_iq1 = c_v1 + iqw
    c_ik1 = c_iq1 + IDX_DIM
    c_iw1 = c_ik1 + IDX_HEADS
    c_glu1 = c_iw1 + 2 * conv_ch
    c_gate1 = c_glu1 + 2 * d
    ua_col, ug_col = d, d + conv_ch
    ga_col = d + 2 * conv_ch
    gc_col = ga_col + d
    k_col = gc_col + d
    v_col = k_col + kvw

    tab_h = _rope_tables(positions, ROPE_DIM, HEAD_DIM)
    tab_i = _rope_tables(positions, IDX_ROPE_DIM, IDX_DIM)

    tm = min(1024, m)
    xf = x.reshape(m, d)
    for l in range(depth):
        wl = w_in[l]
        w_slab = jnp.concatenate(
            [wl[:, :c_q1], wl[:, c_iw1:c_glu1], wl[:, c_glu1:c_gate1], wl[:, c_q1:c_v1]],
            axis=1).astype(bf)
        w_idx = jnp.pad(wl[:, c_v1:c_iw1], ((0, 0), (0, LANES - IDX_DIM - IDX_HEADS))).astype(bf)
        g_mix = norm_mix_g[l][None, :]

        proj = _norm_proj(xf, g_mix, w_slab, bf, tm, 1024).reshape(b, s, -1)
        fproj = _norm_proj(xf, g_mix, w_idx, jnp.float32, tm, w_idx.shape[1]).reshape(b, s, -1)

        q, k, vt, iq, iklo, ikhi, iwp = _prep(
            proj, fproj, tab_h, tab_i,
            _pad_lanes(idx_k_ln_g[l], LANES), _pad_lanes(idx_k_ln_b[l], LANES),
            n_heads, k_col, v_col, TQ)
        iwt = jnp.swapaxes(iwp[:, :, IDX_DIM:IDX_DIM + IDX_HEADS], 1, 2)
        attn = _dsa(q, iq, iwt, iklo, ikhi, k, vt)

        dw = jnp.pad(conv_dw_w[l], ((0, HALO - CONV_WIDTH), (0, 0)))
        u = _conv_module(proj, ua_col, ug_col, dw, conv_dw_b[l][None, :],
                         conv_ln_g[l][None, :], conv_ln_b[l][None, :], TQ)

        y = _mix(u.reshape(m, conv_ch), w_conv_out[l].astype(bf), b_conv_out[l][None, :],
                 proj.reshape(m, -1), ga_col, gc_col, attn.reshape(m, d), tm, 512)
        xf = _mm_res(y, w_out[l].astype(bf), xf, tm, 512)

        t = _ffn_up(xf, norm_ffn_g[l][None, :], w_ffn_gate[l].astype(bf), w_ffn_up[l].astype(bf), tm, 512)
        xf = _mm_res(t, w_ffn_down[l].astype(bf), xf, tm, 512)

    out = _final_norm(xf, final_norm_g[None, :], tm)
    return out.reshape(b, s, d)
```

```python
import functools
import math

import numpy as np
import jax
import jax.numpy as jnp
from jax import lax
from jax.experimental import pallas as pl
from jax.experimental.pallas import tpu as pltpu

HEAD_DIM = 128
N_KV_HEADS = 4
ROPE_DIM = HEAD_DIM // 4
ROPE_THETA = 500000.0
IDX_HEADS = 16
IDX_DIM = 64
IDX_ROPE_DIM = IDX_DIM // 4
TOPK_MAX = 256
CONV_WIDTH = 31
EPS = 1e-6

LANES = 128
SUBLANES = 8
BF16_ROWS = 16
VD = HEAD_DIM + BF16_ROWS
VMEM_LIMIT = 56 * 1024 * 1024

TQ = 256
TK = 1024
TKI = 128
P1_BLOCKS = 4
UNIT_HEADS = 2
CNT_ROWS = 256
CNT_ACC = 32
HALO = 32
MASK_NEG = -1e30
DEN_LIMIT = 2.0 ** 64
F32_LOWEST = float(np.finfo(np.float32).min)
MAX_BISECT = 200

_NT = (((1,), (1,)), ((), ()))


def _cparams(sem):
    return pltpu.CompilerParams(dimension_semantics=sem, vmem_limit_bytes=VMEM_LIMIT)


def _rms_to_bf16(x, g):
    ms = jnp.mean(x * x, axis=-1, keepdims=True)
    return (x * lax.rsqrt(ms + EPS) * g).astype(jnp.bfloat16)


def _norm_proj_kernel(x_ref, g_ref, w_ref, o_ref, h_scr):
    @pl.when(pl.program_id(1) == 0)
    def _():
        h_scr[...] = _rms_to_bf16(x_ref[...], g_ref[...])

    o_ref[...] = jnp.dot(h_scr[...], w_ref[...],
                         preferred_element_type=jnp.float32).astype(o_ref.dtype)


def _norm_proj(x, g, w, out_dtype, tm, tn):
    m, d = x.shape
    n = w.shape[1]
    return pl.pallas_call(
        _norm_proj_kernel,
        grid=(m // tm, n // tn),
        in_specs=[pl.BlockSpec((tm, d), lambda i, j: (i, 0)),
                  pl.BlockSpec((1, d), lambda i, j: (0, 0)),
                  pl.BlockSpec((d, tn), lambda i, j: (0, j))],
        out_specs=pl.BlockSpec((tm, tn), lambda i, j: (i, j)),
        out_shape=jax.ShapeDtypeStruct((m, n), out_dtype),
        scratch_shapes=[pltpu.VMEM((tm, d), jnp.bfloat16)],
        compiler_params=_cparams(("parallel", "arbitrary")),
        name="norm_proj",
    )(x, g, w)


def _ffn_up_kernel(x_ref, g_ref, wg_ref, wu_ref, o_ref, h_scr):
    @pl.when(pl.program_id(1) == 0)
    def _():
        h_scr[...] = _rms_to_bf16(x_ref[...], g_ref[...])

    h = h_scr[...]
    a = jnp.dot(h, wg_ref[...], preferred_element_type=jnp.float32)
    b = jnp.dot(h, wu_ref[...], preferred_element_type=jnp.float32)
    o_ref[...] = (a * jax.nn.sigmoid(a) * b).astype(o_ref.dtype)


def _ffn_up(x, g, wg, wu, tm, tn):
    m, d = x.shape
    n = wg.shape[1]
    return pl.pallas_call(
        _ffn_up_kernel,
        grid=(m // tm, n // tn),
        in_specs=[pl.BlockSpec((tm, d), lambda i, j: (i, 0)),
                  pl.BlockSpec((1, d), lambda i, j: (0, 0)),
                  pl.BlockSpec((d, tn), lambda i, j: (0, j)),
                  pl.BlockSpec((d, tn), lambda i, j: (0, j))],
        out_specs=pl.BlockSpec((tm, tn), lambda i, j: (i, j)),
        out_shape=jax.ShapeDtypeStruct((m, n), jnp.bfloat16),
        scratch_shapes=[pltpu.VMEM((tm, d), jnp.bfloat16)],
        compiler_params=_cparams(("parallel", "arbitrary")),
        name="ffn_up",
    )(x, g, wg, wu)


def _mm_res_kernel(a_ref, w_ref, r_ref, o_ref):
    o_ref[...] = r_ref[...] + jnp.dot(a_ref[...], w_ref[...],
                                      preferred_element_type=jnp.float32)


def _mm_res(a, w, res, tm, tn):
    m, k = a.shape
    n = w.shape[1]
    return pl.pallas_call(
        _mm_res_kernel,
        grid=(m // tm, n // tn),
        in_specs=[pl.BlockSpec((tm, k), lambda i, j: (i, 0)),
                  pl.BlockSpec((k, tn), lambda i, j: (0, j)),
                  pl.BlockSpec((tm, tn), lambda i, j: (i, j))],
        out_specs=pl.BlockSpec((tm, tn), lambda i, j: (i, j)),
        out_shape=jax.ShapeDtypeStruct((m, n), jnp.float32),
        compiler_params=_cparams(("parallel", "arbitrary")),
        name="mm_res",
    )(a, w, res)


def _mix_kernel(u_ref, w_ref, b_ref, ga_ref, gc_ref, at_ref, y_ref):
    conv = jnp.dot(u_ref[...], w_ref[...], preferred_element_type=jnp.float32) + b_ref[...]
    ga = jax.nn.sigmoid(ga_ref[...].astype(jnp.float32))
    gc = jax.nn.sigmoid(gc_ref[...].astype(jnp.float32))
    y_ref[...] = (ga * at_ref[...].astype(jnp.float32) + gc * conv).astype(y_ref.dtype)


def _mix(u, w, b, proj, ga_col, gc_col, attn, tm, tn):
    m, k = u.shape
    n = w.shape[1]
    ga_blk = ga_col // tn
    gc_blk = gc_col // tn
    return pl.pallas_call(
        _mix_kernel,
        grid=(m // tm, n // tn),
        in_specs=[pl.BlockSpec((tm, k), lambda i, j: (i, 0)),
                  pl.BlockSpec((k, tn), lambda i, j: (0, j)),
                  pl.BlockSpec((1, tn), lambda i, j: (0, j)),
                  pl.BlockSpec((tm, tn), lambda i, j: (i, ga_blk + j)),
                  pl.BlockSpec((tm, tn), lambda i, j: (i, gc_blk + j)),
                  pl.BlockSpec((tm, tn), lambda i, j: (i, j))],
        out_specs=pl.BlockSpec((tm, tn), lambda i, j: (i, j)),
        out_shape=jax.ShapeDtypeStruct((m, n), jnp.bfloat16),
        compiler_params=_cparams(("parallel", "arbitrary")),
        name="gated_mix",
    )(u, w, b, proj, proj, attn)


def _final_norm_kernel(x_ref, g_ref, o_ref):
    x = x_ref[...]
    ms = jnp.mean(x * x, axis=-1, keepdims=True)
    o_ref[...] = x * lax.rsqrt(ms + EPS) * g_ref[...]


def _final_norm(x, g, tm):
    m, d = x.shape
    return pl.pallas_call(
        _final_norm_kernel,
        grid=(m // tm,),
        in_specs=[pl.BlockSpec((tm, d), lambda i: (i, 0)),
                  pl.BlockSpec((1, d), lambda i: (0, 0))],
        out_specs=pl.BlockSpec((tm, d), lambda i: (i, 0)),
        out_shape=jax.ShapeDtypeStruct((m, d), jnp.float32),
        compiler_params=_cparams(("parallel",)),
        name="final_norm",
    )(x, g)


def _rope(x, c, s1, s2, half):
    return (x * c + pltpu.roll(x, LANES - half, axis=1) * s1
            + pltpu.roll(x, half, axis=1) * s2)


def _prep_kernel(q_ref, k_ref, v_ref, f_ref, th_ref, ti_ref, lng_ref, lnb_ref,
                 qo_ref, ko_ref, vto_ref, iqo_ref, iklo_ref, ikhi_ref, iwo_ref,
                 *, n_heads, q_scale, iw_scale):
    ch, s1h, s2h = th_ref[0], th_ref[1], th_ref[2]
    ci, s1i, s2i = ti_ref[0], ti_ref[1], ti_ref[2]
    hh = ROPE_DIM // 2
    hi = IDX_ROPE_DIM // 2
    group = n_heads // N_KV_HEADS
    tm = q_ref.shape[0]
    for h in range(n_heads):
        xh = q_ref[:, h * HEAD_DIM:(h + 1) * HEAD_DIM].astype(jnp.float32)
        j = h % group
        qo_ref[h // group, j * tm:(j + 1) * tm, :] = (
            _rope(xh, ch, s1h, s2h, hh) * q_scale).astype(qo_ref.dtype)
    for g in range(N_KV_HEADS):
        xk = k_ref[:, g * HEAD_DIM:(g + 1) * HEAD_DIM].astype(jnp.float32)
        ko_ref[g] = _rope(xk, ch, s1h, s2h, hh).astype(ko_ref.dtype)
        xv = v_ref[:, g * HEAD_DIM:(g + 1) * HEAD_DIM].astype(jnp.float32)
        vto_ref[g, 0:HEAD_DIM, :] = xv.T.astype(vto_ref.dtype)
        vto_ref[g, HEAD_DIM:VD, :] = jnp.ones((BF16_ROWS, tm), vto_ref.dtype)
    n_pairs = IDX_HEADS * IDX_DIM // LANES
    for p in range(n_pairs):
        xi = f_ref[:, p * LANES:(p + 1) * LANES]
        iqo_ref[:, p * LANES:(p + 1) * LANES] = _rope(xi, ci, s1i, s2i, hi).astype(iqo_ref.dtype)
    xt = f_ref[:, n_pairs * LANES:(n_pairs + 1) * LANES]
    lane = lax.broadcasted_iota(jnp.int32, xt.shape, 1)
    is_ik = lane < IDX_DIM
    mu = jnp.sum(jnp.where(is_ik, xt, 0.0), axis=-1, keepdims=True) * (1.0 / IDX_DIM)
    xc = jnp.where(is_ik, xt - mu, 0.0)
    var = jnp.sum(xc * xc, axis=-1, keepdims=True) * (1.0 / IDX_DIM)
    y = xc * lax.rsqrt(var + EPS) * lng_ref[...] + lnb_ref[...]
    r = _rope(y, ci, s1i, s2i, hi)
    iklo_ref[...] = r.astype(iklo_ref.dtype)
    ikhi_ref[...] = pltpu.roll(r, IDX_DIM, axis=1).astype(ikhi_ref.dtype)
    iwo_ref[...] = xt * iw_scale


def _prep(proj, fproj, tab_h, tab_i, lng, lnb, n_heads, k_col, v_col, tm):
    b, s, _ = proj.shape
    d = n_heads * HEAD_DIM
    group = n_heads // N_KV_HEADS
    kvw = N_KV_HEADS * HEAD_DIM
    fw = fproj.shape[-1]
    iqw = IDX_HEADS * IDX_DIM
    kern = functools.partial(
        _prep_kernel, n_heads=n_heads,
        q_scale=(HEAD_DIM ** -0.5) * math.log2(math.e),
        iw_scale=(IDX_HEADS ** -0.5) * (IDX_DIM ** -0.5))
    bf = jnp.bfloat16
    return pl.pallas_call(
        kern,
        grid=(b, s // tm),
        in_specs=[pl.BlockSpec((None, tm, d), lambda bi, i: (bi, i, 0)),
                  pl.BlockSpec((None, tm, kvw), lambda bi, i: (bi, i, k_col // kvw)),
                  pl.BlockSpec((None, tm, kvw), lambda bi, i: (bi, i, v_col // kvw)),
                  pl.BlockSpec((None, tm, fw), lambda bi, i: (bi, i, 0)),
                  pl.BlockSpec((None, 3, tm, LANES), lambda bi, i: (bi, 0, i, 0)),
                  pl.BlockSpec((None, 3, tm, LANES), lambda bi, i: (bi, 0, i, 0)),
                  pl.BlockSpec((1, LANES), lambda bi, i: (0, 0)),
                  pl.BlockSpec((1, LANES), lambda bi, i: (0, 0))],
        out_specs=[pl.BlockSpec((None, N_KV_HEADS, None, group * tm, HEAD_DIM),
                                lambda bi, i: (bi, 0, i, 0, 0)),
                   pl.BlockSpec((None, N_KV_HEADS, tm, HEAD_DIM), lambda bi, i: (bi, 0, i, 0)),
                   pl.BlockSpec((None, N_KV_HEADS, VD, tm), lambda bi, i: (bi, 0, 0, i)),
                   pl.BlockSpec((None, tm, iqw), lambda bi, i: (bi, i, 0)),
                   pl.BlockSpec((None, tm, LANES), lambda bi, i: (bi, i, 0)),
                   pl.BlockSpec((None, tm, LANES), lambda bi, i: (bi, i, 0)),
                   pl.BlockSpec((None, tm, LANES), lambda bi, i: (bi, i, 0))],
        out_shape=[jax.ShapeDtypeStruct((b, N_KV_HEADS, s // tm, group * tm, HEAD_DIM), bf),
                   jax.ShapeDtypeStruct((b, N_KV_HEADS, s, HEAD_DIM), bf),
                   jax.ShapeDtypeStruct((b, N_KV_HEADS, VD, s), bf),
                   jax.ShapeDtypeStruct((b, s, iqw), bf),
                   jax.ShapeDtypeStruct((b, s, LANES), bf),
                   jax.ShapeDtypeStruct((b, s, LANES), bf),
                   jax.ShapeDtypeStruct((b, s, LANES), jnp.float32)],
        compiler_params=_cparams(("parallel", "parallel")),
        name="rope_prep",
    )(proj, proj, proj, fproj, tab_h, tab_i, lng, lnb)


def _dsa_kernel(bi_ref, qi_ref, kt_ref,
                q_ref, iq_ref, iwt_ref, iklo_ref, ikhi_ref, k_ref, vt_ref,
                o_ref,
                sc_ref, stat_ref, bias_ref, thr_ref, m_ref, acc_ref, redo_ref, flag_ref,
                *, n_heads, topk):
    t = pl.program_id(0)
    qi = qi_ref[t]
    kt = kt_ref[t]
    n_keys = (qi + 1) * TQ
    n_kt = (n_keys + TK - 1) // TK
    group = n_heads // N_KV_HEADS
    f32 = jnp.float32

    def score_chunk(r0):
        lo = iklo_ref[pl.ds(r0, TKI), :]
        hi = ikhi_ref[pl.ds(r0, TKI), :]
        acc = jnp.zeros((TKI, TQ), f32)
        for p in range(IDX_HEADS // 2):
            w = iq_ref[:, p * LANES:(p + 1) * LANES]
            ze = lax.dot_general(lo, w, _NT, preferred_element_type=f32)
            zo = lax.dot_general(hi, w, _NT, preferred_element_type=f32)
            acc = acc + iwt_ref[2 * p:2 * p + 1, :] * jnp.maximum(ze, 0.0)
            acc = acc + iwt_ref[2 * p + 1:2 * p + 2, :] * jnp.maximum(zo, 0.0)
        return acc

    def fold8(x, op):
        return op(x.reshape(x.shape[0] // SUBLANES, SUBLANES, x.shape[1]), axis=0)

    @pl.when(kt == 0)
    def _index_and_select():
        chunks_per_block = TQ // TKI

        def score_blocks(first_block, n_blocks):
            mx, mn = stat_ref[0], stat_ref[1]
            for ch in range(n_blocks * chunks_per_block):
                r0 = pl.multiple_of(first_block * TQ + ch * TKI, TKI)
                sc = score_chunk(r0)
                sc_ref[pl.ds(r0, TKI), :] = sc
                mx = jnp.maximum(mx, fold8(sc, jnp.max))
                mn = jnp.minimum(mn, fold8(sc, jnp.min))
            stat_ref[0], stat_ref[1] = mx, mn

        stat_ref[0] = jnp.full((SUBLANES, TQ), -jnp.inf, f32)
        stat_ref[1] = jnp.full((SUBLANES, TQ), jnp.inf, f32)

        def many_blocks(c, carry):
            score_blocks(c * P1_BLOCKS, P1_BLOCKS)
            return carry

        lax.fori_loop(0, qi // P1_BLOCKS, many_blocks, 0)
        n_blocks = P1_BLOCKS // 2
        while n_blocks >= 1:
            @pl.when((qi & n_blocks) != 0)
            def _(n_blocks=n_blocks):
                score_blocks(qi // (2 * n_blocks) * (2 * n_blocks), n_blocks)
            n_blocks //= 2

        mx, mn = stat_ref[0], stat_ref[1]
        for dchunk in range(chunks_per_block):
            r0 = pl.multiple_of(qi * TQ + dchunk * TKI, TKI)
            sc = score_chunk(r0)
            row = lax.broadcasted_iota(jnp.int32, (TKI, TQ), 0) + dchunk * TKI
            col = lax.broadcasted_iota(jnp.int32, (TKI, TQ), 1)
            valid = row <= col
            sc_ref[pl.ds(r0, TKI), :] = jnp.where(valid, sc, -jnp.inf)
            mx = jnp.maximum(mx, fold8(jnp.where(valid, sc, -jnp.inf), jnp.max))
            mn = jnp.minimum(mn, fold8(jnp.where(valid, sc, jnp.inf), jnp.min))

        def fill_past(c, carry):
            sc_ref[pl.ds(pl.multiple_of(n_keys + c * TQ, TQ), TQ), :] = jnp.full((TQ, TQ), -jnp.inf, f32)
            return carry

        lax.fori_loop(0, (n_kt * TK - n_keys) // TQ, fill_past, 0)

        top = jnp.max(mx, axis=0, keepdims=True)
        hi0 = top + jnp.maximum(jnp.abs(top) * f32(2.0 ** -20), f32(1e-30))
        lo0 = jnp.min(mn, axis=0, keepdims=True)
        lane = lax.broadcasted_iota(jnp.int32, (1, TQ), 1)
        n_valid = qi * TQ + lane + 1
        take_all = n_valid <= topk
        kf = f32(topk)
        n_cnt = n_keys // CNT_ROWS

        def count_ge(mid):
            def body(c, cnt):
                r0 = pl.multiple_of(c * CNT_ROWS, CNT_ROWS)
                ind = jnp.where(sc_ref[pl.ds(r0, CNT_ROWS), :] >= mid, f32(1), f32(0))
                return cnt + jnp.sum(ind.reshape(CNT_ROWS // CNT_ACC, CNT_ACC, TQ), axis=0)
            cnt = lax.fori_loop(0, n_cnt, body, jnp.zeros((CNT_ACC, TQ), f32))
            return jnp.sum(cnt, axis=0, keepdims=True)

        def cond(st):
            return jnp.logical_and(st[0] < MAX_BISECT, st[1] > 0)

        def max_below(hi):
            def body(c, acc):
                r0 = pl.multiple_of(c * CNT_ROWS, CNT_ROWS)
                blk = sc_ref[pl.ds(r0, CNT_ROWS), :]
                val = jnp.where(blk < hi, blk, -jnp.inf)
                return jnp.maximum(acc, jnp.max(val.reshape(CNT_ROWS // CNT_ACC, CNT_ACC, TQ), axis=0))
            acc = lax.fori_loop(0, n_cnt, body, jnp.full((CNT_ACC, TQ), -jnp.inf, f32))
            return jnp.max(acc, axis=0, keepdims=True)

        def n_open(done, clo, chi):
            still = jnp.logical_and(done < 0.5, (clo - chi) > 2.0)
            return jnp.sum(jnp.where(still, jnp.int32(1), jnp.int32(0)))

        def body(st):
            it, _, lo, hi, clo, chi, thr, done = st
            mid = lo * 0.5 + hi * 0.5
            stuck = jnp.logical_or(mid <= lo, mid >= hi)
            cnt = count_ge(mid)
            exact = cnt == kf
            ge = cnt >= kf
            act = done < 0.5
            thr = jnp.where(jnp.logical_and(act, exact), mid,
                            jnp.where(jnp.logical_and(act, stuck), lo, thr))
            newly = jnp.logical_and(act, jnp.logical_or(exact, stuck))
            move = jnp.logical_and(act, jnp.logical_not(newly))
            up = jnp.logical_and(move, ge)
            down = jnp.logical_and(move, jnp.logical_not(ge))
            lo, clo = jnp.where(up, mid, lo), jnp.where(up, cnt, clo)
            hi, chi = jnp.where(down, mid, hi), jnp.where(down, cnt, chi)
            done = jnp.where(newly, f32(1), done)
            return it + 1, n_open(done, clo, chi), lo, hi, clo, chi, thr, done

        done0 = jnp.where(take_all, f32(1), f32(0))
        thr0 = jnp.full((1, TQ), F32_LOWEST, f32)
        clo0 = n_valid.astype(f32)
        chi0 = jnp.zeros((1, TQ), f32)
        st = lax.while_loop(cond, body, (jnp.int32(0), n_open(done0, clo0, chi0),
                                         lo0, hi0, clo0, chi0, thr0, done0))
        hi, thr, done = st[3], st[6], st[7]
        thr_ref[...] = jnp.where(done < 0.5, max_below(hi), thr)

        m_ref[...] = jnp.full(m_ref.shape, MASK_NEG, f32)
        acc_ref[...] = jnp.zeros(acc_ref.shape, f32)

    k0 = pl.multiple_of(kt * TK, TK)
    bias_ref[...] = jnp.where(sc_ref[pl.ds(k0, TK), :] >= thr_ref[...], f32(0), f32(MASK_NEG))

    def logits(g):
        s = lax.dot_general(k_ref[g], q_ref[g], _NT, preferred_element_type=f32)
        return s + jnp.tile(bias_ref[...], (1, group))

    @pl.when(kt == 0)
    def _():
        redo_ref[...] = jnp.ones(redo_ref.shape, f32)
        flag_ref[0] = jnp.int32(1)

    @pl.when(kt > 0)
    def _fixed_base_tile():
        width = UNIT_HEADS * TQ
        per_group = group // UNIT_HEADS
        n_units = N_KV_HEADS * per_group

        def unit_logits(u):
            g, c0 = u // per_group, (u % per_group) * width
            s = lax.dot_general(k_ref[g], q_ref[g, c0:c0 + width, :], _NT, preferred_element_type=f32)
            return s + jnp.tile(bias_ref[...], (1, UNIT_HEADS))

        bad = jnp.zeros((1, width), f32)
        s_next = unit_logits(0)
        for u in range(n_units):
            g, c0 = u // per_group, (u % per_group) * width
            s_cur = s_next
            if u + 1 < n_units:
                s_next = unit_logits(u + 1)
            p = jnp.exp2(s_cur - m_ref[g, :, c0:c0 + width]).astype(jnp.bfloat16)
            pv = jnp.dot(vt_ref[g], p, preferred_element_type=f32)
            acc = acc_ref[g, :, c0:c0 + width]
            ok = (acc[HEAD_DIM:HEAD_DIM + 1, :] + pv[HEAD_DIM:HEAD_DIM + 1, :]) < DEN_LIMIT
            acc_ref[g, :, c0:c0 + width] = jnp.where(ok, acc + pv, acc)
            redo = jnp.where(ok, f32(0), f32(1))
            redo_ref[g, :, c0:c0 + width] = redo
            bad = jnp.maximum(bad, redo)
        flag_ref[0] = (jnp.max(bad) > 0.0).astype(jnp.int32)

    @pl.when(flag_ref[0] == 1)
    def _online_max_tile():
        def softmax_pv(g, s):
            redo = redo_ref[g] > 0.5
            m_prev = m_ref[g]
            m_new = jnp.where(redo, jnp.maximum(m_prev, jnp.max(s, axis=0, keepdims=True)), m_prev)
            alpha = jnp.exp2(m_prev - m_new)
            p = jnp.exp2(s - m_new).astype(jnp.bfloat16)
            pv = jnp.dot(vt_ref[g], p, preferred_element_type=f32)
            acc = acc_ref[g]
            acc_ref[g] = jnp.where(redo, alpha * acc + pv, acc)
            m_ref[g] = m_new

        s_next = logits(0)
        for g in range(N_KV_HEADS):
            s_cur = s_next
            if g + 1 < N_KV_HEADS:
                s_next = logits(g + 1)
            softmax_pv(g, s_cur)

    @pl.when(kt == n_kt - 1)
    def _finalize():
        for g in range(N_KV_HEADS):
            a = acc_ref[g]
            o = a[:HEAD_DIM] * (1.0 / a[HEAD_DIM:HEAD_DIM + 1])
            for j in range(group):
                h = g * group + j
                o_ref[:, h * HEAD_DIM:(h + 1) * HEAD_DIM] = o[:, j * TQ:(j + 1) * TQ].T.astype(o_ref.dtype)


def _dsa_steps(b, s):
    bi, qi, kt = [], [], []
    for bb in range(b):
        for q in range(s // TQ):
            for k in range(-(-((q + 1) * TQ) // TK)):
                bi.append(bb)
                qi.append(q)
                kt.append(k)
    return (np.asarray(bi, np.int32), np.asarray(qi, np.int32), np.asarray(kt, np.int32))


def _dsa(q, iq, iwt, iklo, ikhi, k, vt):
    b, _, nq, gtq, _ = q.shape
    group = gtq // TQ
    n_heads = N_KV_HEADS * group
    s = nq * TQ
    assert TK % TQ == 0 and TQ % TKI == 0 and TQ % CNT_ROWS == 0 and s % TK == 0
    topk = min(TOPK_MAX, s // 4)
    bi, qi, kt = _dsa_steps(b, s)
    iqw = iq.shape[-1]
    kern = functools.partial(_dsa_kernel, n_heads=n_heads, topk=topk)
    grid_spec = pltpu.PrefetchScalarGridSpec(
        num_scalar_prefetch=3,
        grid=(len(bi),),
        in_specs=[
            pl.BlockSpec((None, N_KV_HEADS, None, gtq, HEAD_DIM),
                         lambda t, b_, q_, k_: (b_[t], 0, q_[t], 0, 0)),
            pl.BlockSpec((None, TQ, iqw), lambda t, b_, q_, k_: (b_[t], q_[t], 0)),
            pl.BlockSpec((None, IDX_HEADS, TQ), lambda t, b_, q_, k_: (b_[t], 0, q_[t])),
            pl.BlockSpec((None, s, LANES), lambda t, b_, q_, k_: (b_[t], 0, 0)),
            pl.BlockSpec((None, s, LANES), lambda t, b_, q_, k_: (b_[t], 0, 0)),
            pl.BlockSpec((None, N_KV_HEADS, TK, HEAD_DIM), lambda t, b_, q_, k_: (b_[t], 0, k_[t], 0)),
            pl.BlockSpec((None, N_KV_HEADS, VD, TK), lambda t, b_, q_, k_: (b_[t], 0, 0, k_[t])),
        ],
        out_specs=pl.BlockSpec((None, TQ, n_heads * HEAD_DIM), lambda t, b_, q_, k_: (b_[t], q_[t], 0)),
        scratch_shapes=[
            pltpu.VMEM((s, TQ), jnp.float32),
            pltpu.VMEM((2, SUBLANES, TQ), jnp.float32),
            pltpu.VMEM((TK, TQ), jnp.float32),
            pltpu.VMEM((1, TQ), jnp.float32),
            pltpu.VMEM((N_KV_HEADS, 1, gtq), jnp.float32),
            pltpu.VMEM((N_KV_HEADS, VD, gtq), jnp.float32),
            pltpu.VMEM((N_KV_HEADS, 1, gtq), jnp.float32),
            pltpu.SMEM((1,), jnp.int32),
        ],
    )
    return pl.pallas_call(
        kern,
        grid_spec=grid_spec,
        out_shape=jax.ShapeDtypeStruct((b, s, n_heads * HEAD_DIM), jnp.bfloat16),
        compiler_params=_cparams(("arbitrary",)),
        name="dsa_attention",
    )(jnp.asarray(bi), jnp.asarray(qi), jnp.asarray(kt), q, iq, iwt, iklo, ikhi, k, vt)


CONV_ROWS = 64
CONV_COLS = 256


def _conv_kernel(ua_ref, ug_ref, ha_ref, hg_ref, w_ref, b_ref, g_ref, beta_ref, o_ref,
                 glu_scr, y_scr, h_scr, *, ts):
    i = pl.program_id(1)
    f32 = jnp.float32
    ua = ua_ref[...].astype(f32)
    ug = ug_ref[...].astype(f32)
    glu_scr[HALO:HALO + ts, :] = ua * jax.nn.sigmoid(ug)
    ha = ha_ref[...].astype(f32)
    hg = hg_ref[...].astype(f32)
    halo = ha * jax.nn.sigmoid(hg)
    glu_scr[0:HALO, :] = jnp.where(i > 0, halo, 0.0)

    c = y_scr.shape[1]
    off = HALO - (CONV_WIDTH - 1)
    for cs in range(0, c, CONV_COLS):
        cols = slice(cs, cs + CONV_COLS)
        wcs = w_ref[:, cols]
        for r0 in range(0, ts, CONV_ROWS):
            acc = jnp.zeros((CONV_ROWS, CONV_COLS), f32) + b_ref[:, cols]
            for sh in range(SUBLANES):
                rows = CONV_ROWS + (SUBLANES if sh else 0)
                part = None
                for j in range(CONV_WIDTH):
                    if (off + j) % SUBLANES != sh:
                        continue
                    a0 = r0 + (off + j) // SUBLANES * SUBLANES
                    term = wcs[j:j + 1, :] * glu_scr[a0:a0 + rows, cols]
                    part = term if part is None else part + term
                if sh == 0:
                    acc = acc + part
                else:
                    h_scr[sh, :, :] = part
                    acc = acc + h_scr[sh, sh:sh + CONV_ROWS, :]
            y_scr[r0:r0 + CONV_ROWS, cols] = acc

    y = y_scr[...]
    mu = jnp.mean(y, axis=-1, keepdims=True)
    yc = y - mu
    var = jnp.mean(yc * yc, axis=-1, keepdims=True)
    z = yc * lax.rsqrt(var + EPS) * g_ref[...] + beta_ref[...]
    o_ref[...] = (z * jax.nn.sigmoid(z)).astype(o_ref.dtype)


def _conv_module(proj, ua_col, ug_col, w, bias, g, beta, ts):
    b, s, _ = proj.shape
    c = w.shape[1]
    hb = ts // HALO
    kern = functools.partial(_conv_kernel, ts=ts)
    return pl.pallas_call(
        kern,
        grid=(b, s // ts),
        in_specs=[pl.BlockSpec((None, ts, c), lambda bi, i: (bi, i, ua_col // c)),
                  pl.BlockSpec((None, ts, c), lambda bi, i: (bi, i, ug_col // c)),
                  pl.BlockSpec((None, HALO, c), lambda bi, i: (bi, jnp.maximum(i * hb - 1, 0), ua_col // c)),
                  pl.BlockSpec((None, HALO, c), lambda bi, i: (bi, jnp.maximum(i * hb - 1, 0), ug_col // c)),
                  pl.BlockSpec((HALO, c), lambda bi, i: (0, 0)),
                  pl.BlockSpec((1, c), lambda bi, i: (0, 0)),
                  pl.BlockSpec((1, c), lambda bi, i: (0, 0)),
                  pl.BlockSpec((1, c), lambda bi, i: (0, 0))],
        out_specs=pl.BlockSpec((None, ts, c), lambda bi, i: (bi, i, 0)),
        out_shape=jax.ShapeDtypeStruct((b, s, c), jnp.bfloat16),
        scratch_shapes=[pltpu.VMEM((HALO + ts, c), jnp.float32),
                        pltpu.VMEM((ts, c), jnp.float32),
                        pltpu.VMEM((SUBLANES, CONV_ROWS + SUBLANES, CONV_COLS), jnp.float32)],
        compiler_params=_cparams(("parallel", "arbitrary")),
        name="conv_module",
    )(proj, proj, proj, proj, w, bias, g, beta)


def _rope_tables(pos, rot_dim, period):
    half = rot_dim // 2
    freqs = ROPE_THETA ** (-jnp.arange(half, dtype=jnp.float32) / half)
    ang = pos.astype(jnp.float32)[..., None] * freqs
    cos, sin = jnp.cos(ang), jnp.sin(ang)
    pad = period - rot_dim
    z = jnp.zeros(ang.shape[:2] + (pad,), jnp.float32)
    zh = jnp.zeros_like(sin)
    c = jnp.concatenate([cos, cos, z + 1.0], axis=-1)
    s1 = jnp.concatenate([-sin, zh, z], axis=-1)
    s2 = jnp.concatenate([zh, sin, z], axis=-1)
    tab = jnp.stack([c, s1, s2], axis=1)
    return jnp.tile(tab, (1, 1, 1, LANES // period))


def _pad_lanes(v, width):
    return jnp.pad(v, (0, width - v.shape[0]))[None, :].astype(jnp.float32)


def kernel(x, positions, norm_mix_g, w_in, idx_k_ln_g, idx_k_ln_b, conv_dw_w, conv_dw_b, conv_ln_g, conv_ln_b, w_conv_out, b_conv_out, w_out, norm_ffn_g, w_ffn_gate, w_ffn_up, w_ffn_down, final_norm_g):
    b, s, d = x.shape
    depth = w_in.shape[0]
    n_heads = d // HEAD_DIM
    kvw = N_KV_HEADS * HEAD_DIM
    iqw = IDX_HEADS * IDX_DIM
    conv_ch = conv_dw_w.shape[-1]
    m = b * s
    bf = jnp.bfloat16

    c_q1 = d
    c_k1 = c_q1 + kvw
    c_v1 = c_k1 + kvw
    c_iq1 = c_v1 + iqw
    c_ik1 = c_iq1 + IDX_DIM
    c_iw1 = c_ik1 + IDX_HEADS
    c_glu1 = c_iw1 + 2 * conv_ch
    c_gate1 = c_glu1 + 2 * d
    ua_col, ug_col = d, d + conv_ch
    ga_col = d + 2 * conv_ch
    gc_col = ga_col + d
    k_col = gc_col + d
    v_col = k_col + kvw

    tab_h = _rope_tables(positions, ROPE_DIM, HEAD_DIM)
    tab_i = _rope_tables(positions, IDX_ROPE_DIM, IDX_DIM)

    tm = min(1024, m)
    xf = x.reshape(m, d)
    for l in range(depth):
        wl = w_in[l]
        w_slab = jnp.concatenate(
            [wl[:, :c_q1], wl[:, c_iw1:c_glu1], wl[:, c_glu1:c_gate1], wl[:, c_q1:c_v1]],
            axis=1).astype(bf)
        w_idx = jnp.pad(wl[:, c_v1:c_iw1], ((0, 0), (0, LANES - IDX_DIM - IDX_HEADS))).astype(bf)
        g_mix = norm_mix_g[l][None, :]

        proj = _norm_proj(xf, g_mix, w_slab, bf, tm, 1024).reshape(b, s, -1)
        fproj = _norm_proj(xf, g_mix, w_idx, jnp.float32, tm, w_idx.shape[1]).reshape(b, s, -1)

        q, k, vt, iq, iklo, ikhi, iwp = _prep(
            proj, fproj, tab_h, tab_i,
            _pad_lanes(idx_k_ln_g[l], LANES), _pad_lanes(idx_k_ln_b[l], LANES),
            n_heads, k_col, v_col, TQ)
        iwt = jnp.swapaxes(iwp[:, :, IDX_DIM:IDX_DIM + IDX_HEADS], 1, 2)
        attn = _dsa(q, iq, iwt, iklo, ikhi, k, vt)

        dw = jnp.pad(conv_dw_w[l], ((0, HALO - CONV_WIDTH), (0, 0)))
        u = _conv_module(proj, ua_col, ug_col, dw, conv_dw_b[l][None, :],
                         conv_ln_g[l][None, :], conv_ln_b[l][None, :], TQ)

        y = _mix(u.reshape(m, conv_ch), w_conv_out[l].astype(bf), b_conv_out[l][None, :],
                 proj.reshape(m, -1), ga_col, gc_col, attn.reshape(m, d), tm, 1024)
        xf = _mm_res(y, w_out[l].astype(bf), xf, tm, 1024)

        t = _ffn_up(xf, norm_ffn_g[l][None, :], w_ffn_gate[l].astype(bf), w_ffn_up[l].astype(bf), tm, 512)
        xf = _mm_res(t, w_ffn_down[l].astype(bf), xf, tm, 512)

    out = _final_norm(xf, final_norm_g[None, :], tm)
    return out.reshape(b, s, d)
```
